```python
import jax, jax.numpy as jnp
from jax import lax
import numpy as np

D_MODEL = 1024
BATCH = 8
SEQ = 4096
DEPTH = 2

HEAD_DIM = 64
A_HEADS = 8
A_KV_RANK = 128
IDX_HEADS = 4
IDX_DIM = 64
IDX_TOPK_MAX = 256
B_HEADS = 8
C_HEADS = 16
MOBA_BLOCK = 256
MOBA_TOPK = 3
Q_BLOCK = 128
D_FF = -(-8 * D_MODEL // (3 * 256)) * 256
DEEPNORM_ALPHA = (2 * DEPTH) ** 0.25
DEEPNORM_BETA = (8 * DEPTH) ** -0.25
EVEN_IN = A_HEADS * HEAD_DIM + A_KV_RANK + IDX_HEADS * IDX_DIM + IDX_DIM + IDX_HEADS + 3 * B_HEADS * HEAD_DIM
ODD_IN = 3 * C_HEADS * HEAD_DIM
NEG = -1e30

kernel_name = "hybrid_dsa_stickbreak_moba_deepnorm"


def layer_norm(x, g, b, eps=1e-5):
    xf = x.astype(jnp.float32)
    mu = xf.mean(-1, keepdims=True)
    var = jnp.square(xf - mu).mean(-1, keepdims=True)
    return ((xf - mu) * lax.rsqrt(var + eps) * g.astype(jnp.float32) + b.astype(jnp.float32)).astype(x.dtype)


def rms_norm(x, g, eps=1e-6):
    xf = x.astype(jnp.float32)
    return (xf * lax.rsqrt(jnp.mean(xf * xf, -1, keepdims=True) + eps) * g.astype(jnp.float32)).astype(x.dtype)


def alibi_slopes(n):
    return jnp.asarray(2.0 ** (-8.0 * np.arange(1, n + 1) / n), dtype=jnp.float32)


def to_qblocks(t):
    B, S = t.shape[:2]
    return t.reshape((B, S // Q_BLOCK, Q_BLOCK) + t.shape[2:]).swapaxes(0, 1)


def from_qblocks(t):
    t = t.swapaxes(0, 1)
    return t.reshape((t.shape[0], t.shape[1] * t.shape[2]) + t.shape[3:])


def dsa_attention(q, c_kv, q_idx, k_idx, w_idx, w_uk, w_uv):
    B, S = q.shape[:2]
    topk = min(IDX_TOPK_MAX, S // 4)
    nq = S // Q_BLOCK
    slopes = alibi_slopes(A_HEADS)
    key_pos = jnp.arange(S)
    q_lat = jnp.einsum('bshd,hdr->bshr', q, w_uk)

    def block(args):
        qb, qib, wb, t0 = args
        tq = t0 + jnp.arange(Q_BLOCK)
        causal = key_pos[None, :] <= tq[:, None]
        dots = jnp.einsum('bqhd,bsd->bqhs', qib, k_idx).astype(jnp.float32) * IDX_DIM ** -0.5
        index = jnp.einsum('bqh,bqhs->bqs', wb.astype(jnp.float32), jax.nn.relu(dots))
        index = jnp.where(causal[None], index, NEG)
        _, sel = lax.top_k(index, topk)
        c_sel = jax.vmap(lambda c, i: c[i])(c_kv, sel)
        valid = sel <= tq[None, :, None]
        dist = (tq[None, :, None] - sel).astype(jnp.float32)
        logits = jnp.einsum('bqhr,bqkr->bqhk', qb, c_sel).astype(jnp.float32) * HEAD_DIM ** -0.5
        logits = logits - slopes[None, None, :, None] * dist[:, :, None, :]
        logits = jnp.where(valid[:, :, None, :], logits, NEG)
        p = jax.nn.softmax(logits, axis=-1).astype(c_sel.dtype)
        o_lat = jnp.einsum('bqhk,bqkr->bqhr', p, c_sel)
        return jnp.einsum('bqhr,hrd->bqhd', o_lat, w_uv)

    out = lax.map(block, (to_qblocks(q_lat), to_qblocks(q_idx), to_qblocks(w_idx),
                          jnp.arange(nq, dtype=jnp.int32) * Q_BLOCK))
    return from_qblocks(out)


def stick_breaking_attention(q, k, v):
    B, S = q.shape[:2]
    nq = S // Q_BLOCK
    key_pos = jnp.arange(S)

    def block(args):
        qb, t0 = args
        tq = t0 + jnp.arange(Q_BLOCK)
        strict = (key_pos[None, :] < tq[:, None])[None, None]
        z = jnp.einsum('bqhd,bshd->bhqs', qb, k).astype(jnp.float32) * HEAD_DIM ** -0.5
        log_beta = jax.nn.log_sigmoid(z)
        log_1m = jnp.where(strict, jax.nn.log_sigmoid(-z), 0.0)
        log_rem = lax.cumsum(log_1m, axis=3, reverse=True) - log_1m
        a = jnp.where(strict, jnp.exp(log_beta + log_rem), 0.0)
        return jnp.einsum('bhqs,bshd->bqhd', a.astype(v.dtype), v)

    out = lax.map(block, (to_qblocks(q), jnp.arange(nq, dtype=jnp.int32) * Q_BLOCK))
    return from_qblocks(out)


def moba_attention(q, k, v):
    B, S, H, D = q.shape
    P = MOBA_BLOCK
    nb = -(-S // P)
    kk = min(MOBA_TOPK, nb)
    pad = nb * P - S
    kp = jnp.pad(k, ((0, 0), (0, pad), (0, 0), (0, 0)))
    vp = jnp.pad(v, ((0, 0), (0, pad), (0, 0), (0, 0)))
    k_blk = kp.reshape(B, nb, P, H, D).transpose(0, 3, 1, 2, 4)
    v_blk = vp.reshape(B, nb, P, H, D).transpose(0, 3, 1, 2, 4)
    k_mean = k_blk.astype(jnp.float32).mean(axis=3)
    slopes = alibi_slopes(H)
    nq = S // Q_BLOCK
    q_chunks = q.reshape(B * nq, Q_BLOCK, H, D)
    b_ids = jnp.repeat(jnp.arange(B, dtype=jnp.int32), nq)
    t0s = jnp.tile(jnp.arange(nq, dtype=jnp.int32) * Q_BLOCK, B)
    hidx = jnp.arange(H)[None, :, None]
    blk_off = jnp.arange(P)

    def block(args):
        qb, b, t0 = args
        kb, vb, km = k_blk[b], v_blk[b], k_mean[b]
        tq = t0 + jnp.arange(Q_BLOCK)
        cur = t0 // P
        gate = jnp.einsum('qhd,hnd->qhn', qb.astype(jnp.float32), km)
        gate = jnp.where((jnp.arange(nb) < cur)[None, None, :], gate, NEG)
        _, sel = lax.top_k(gate, kk)
        k_sel = kb[hidx, sel]
        v_sel = vb[hidx, sel]
        k_own = lax.dynamic_index_in_dim(kb, cur, axis=1, keepdims=False)
        v_own = lax.dynamic_index_in_dim(vb, cur, axis=1, keepdims=False)
        scale = HEAD_DIM ** -0.5
        s_sel = jnp.einsum('qhd,qhrpd->qhrp', qb, k_sel).astype(jnp.float32) * scale
        s_own = jnp.einsum('qhd,hpd->qhp', qb, k_own).astype(jnp.float32) * scale
        pos_sel = sel[..., None] * P + blk_off
        pos_own = cur * P + blk_off
        s_sel = s_sel - slopes[None, :, None, None] * (tq[:, None, None, None] - pos_sel).astype(jnp.float32)
        s_own = s_own - slopes[None, :, None] * (tq[:, None, None] - pos_own[None, None, :]).astype(jnp.float32)
        s_sel = jnp.where((sel < cur)[..., None], s_sel, NEG)
        s_own = jnp.where((pos_own[None, :] <= tq[:, None])[:, None, :], s_own, NEG)
        logits = jnp.concatenate([s_sel.reshape(Q_BLOCK, H, kk * P), s_own], axis=-1)
        p = jax.nn.softmax(logits, axis=-1).astype(v.dtype)
        p_sel = p[..., :kk * P].reshape(Q_BLOCK, H, kk, P)
        p_own = p[..., kk * P:]
        return (jnp.einsum('qhrp,qhrpd->qhd', p_sel, v_sel)
                + jnp.einsum('qhp,hpd->qhd', p_own, v_own))

    out = lax.map(block, (q_chunks, b_ids, t0s))
    return out.reshape(B, S, H, D)


def dsa_stick_mixer(x, w_in, kv_norm, w_uk, w_uv, w_out):
    B, S, _ = x.shape
    widths = (A_HEADS * HEAD_DIM, A_KV_RANK, IDX_HEADS * IDX_DIM, IDX_DIM, IDX_HEADS,
              B_HEADS * HEAD_DIM, B_HEADS * HEAD_DIM, B_HEADS * HEAD_DIM)
    cuts = np.cumsum(widths)[:-1].tolist()
    q_a, c_kv, q_idx, k_idx, w_idx, q_b, k_b, v_b = jnp.split(x @ w_in, cuts, axis=-1)
    o_a = dsa_attention(q_a.reshape(B, S, A_HEADS, HEAD_DIM), rms_norm(c_kv, kv_norm),
                        q_idx.reshape(B, S, IDX_HEADS, IDX_DIM), k_idx,
                        w_idx * IDX_HEADS ** -0.5, w_uk, w_uv)
    o_b = stick_breaking_attention(q_b.reshape(B, S, B_HEADS, HEAD_DIM),
                                   k_b.reshape(B, S, B_HEADS, HEAD_DIM),
                                   v_b.reshape(B, S, B_HEADS, HEAD_DIM))
    o = jnp.concatenate([o_a.reshape(B, S, -1), o_b.reshape(B, S, -1)], axis=-1)
    return o @ w_out


def moba_mixer(x, w_in, w_out):
    B, S, _ = x.shape
    q, k, v = jnp.split(x @ w_in, 3, axis=-1)
    shp = (B, S, C_HEADS, HEAD_DIM)
    o = moba_attention(q.reshape(shp), k.reshape(shp), v.reshape(shp))
    return o.reshape(B, S, -1) @ w_out


def swiglu(x, w_gate, w_up, w_down):
    return (jax.nn.silu(x @ w_gate) * (x @ w_up)) @ w_down


def setup_inputs(seed: int = 0) -> dict:
    key = jax.random.key(seed)
    ks = jax.random.split(key, 16)
    n_even = (DEPTH + 1) // 2
    n_odd = DEPTH // 2
    f32 = jnp.float32
    nrm = lambda k, shp, s: jax.random.normal(k, shp, f32) * s
    return {
        "x": jax.random.normal(ks[0], (BATCH, SEQ, D_MODEL), f32),
        "even_w_in": nrm(ks[1], (n_even, D_MODEL, EVEN_IN), D_MODEL ** -0.5),
        "even_kv_norm": 1.0 + nrm(ks[2], (n_even, A_KV_RANK), 0.05),
        "even_w_uk": nrm(ks[3], (n_even, A_HEADS, HEAD_DIM, A_KV_RANK), A_KV_RANK ** -0.5),
        "even_w_uv": nrm(ks[4], (n_even, A_HEADS, A_KV_RANK, HEAD_DIM), A_KV_RANK ** -0.5),
        "even_w_out": nrm(ks[5], (n_even, D_MODEL, D_MODEL), D_MODEL ** -0.5 * DEEPNORM_BETA),
        "odd_w_in": nrm(ks[6], (n_odd, D_MODEL, ODD_IN), D_MODEL ** -0.5),
        "odd_w_out": nrm(ks[7], (n_odd, D_MODEL, D_MODEL), D_MODEL ** -0.5 * DEEPNORM_BETA),
        "ln_mix_g": 1.0 + nrm(ks[8], (DEPTH, D_MODEL), 0.05),
        "ln_mix_b": nrm(ks[9], (DEPTH, D_MODEL), 0.02),
        "ffn_gate": nrm(ks[10], (DEPTH, D_MODEL, D_FF), D_MODEL ** -0.5),
        "ffn_up": nrm(ks[11], (DEPTH, D_MODEL, D_FF), D_MODEL ** -0.5),
        "ffn_down": nrm(ks[12], (DEPTH, D_FF, D_MODEL), D_FF ** -0.5 * DEEPNORM_BETA),
        "ln_ffn_g": 1.0 + nrm(ks[13], (DEPTH, D_MODEL), 0.05),
        "ln_ffn_b": nrm(ks[14], (DEPTH, D_MODEL), 0.02),
    }


def reference(x, even_w_in, even_kv_norm, even_w_uk, even_w_uv, even_w_out, odd_w_in, odd_w_out,
              ln_mix_g, ln_mix_b, ffn_gate, ffn_up, ffn_down, ln_ffn_g, ln_ffn_b):
    h = x
    for i in range(DEPTH):
        j = i // 2
        if i % 2 == 0:
            m = dsa_stick_mixer(h, even_w_in[j], even_kv_norm[j], even_w_uk[j], even_w_uv[j], even_w_out[j])
        else:
            m = moba_mixer(h, odd_w_in[j], odd_w_out[j])
        h = layer_norm(DEEPNORM_ALPHA * h + m, ln_mix_g[i], ln_mix_b[i])
        f = swiglu(h, ffn_gate[i], ffn_up[i], ffn_down[i])
        h = layer_norm(DEEPNORM_ALPHA * h + f, ln_ffn_g[i], ln_ffn_b[i])
    return h
```

```python
import functools

import numpy as np
import jax
import jax.numpy as jnp
from jax import lax
from jax.experimental import pallas as pl
from jax.experimental.pallas import tpu as pltpu

D_MODEL = 1024
DEPTH = 2
HEAD_DIM = 64
A_HEADS = 8
A_KV_RANK = 128
IDX_HEADS = 4
IDX_DIM = 64
IDX_TOPK_MAX = 256
B_HEADS = 8
C_HEADS = 16
MOBA_BLOCK = 256
MOBA_TOPK = 3
D_FF = -(-8 * D_MODEL // (3 * 256)) * 256
DEEPNORM_ALPHA = (2 * DEPTH) ** 0.25
NEG = -1e30

LANES = 128
VMEM_LIMIT = 52 * 1024 * 1024

BF16 = jnp.bfloat16
F32 = jnp.float32
I32 = jnp.int32


def _dot(a, b):
    return jnp.dot(a, b, preferred_element_type=F32)


def _dot_nt(a, b, precision=None):
    return lax.dot_general(a, b, (((1,), (1,)), ((), ())), precision=precision,
                           preferred_element_type=F32)


def _iota(shape, dim):
    return lax.broadcasted_iota(I32, shape, dim)


def _params(sem, vmem=VMEM_LIMIT):
    return pltpu.CompilerParams(dimension_semantics=sem, vmem_limit_bytes=vmem)


def _const_spec(shape):
    nd = len(shape)
    return pl.BlockSpec(shape, lambda *_: (0,) * nd)


def _layer_norm(y, g, b):
    mu = jnp.mean(y, axis=-1, keepdims=True)
    d = y - mu
    var = jnp.mean(d * d, axis=-1, keepdims=True)
    return d * lax.rsqrt(var + 1e-5) * g + b


def _proj_even_kernel(x_ref, w_ref, wuk_ref, g_ref,
                      qlat_ref, ckv_ref, qidx_ref, kidx_ref, widx_ref, qb_ref, kb_ref, vb_ref):
    xb = x_ref[...].astype(BF16)
    qa = _dot(xb, w_ref[:, 0:512]).astype(BF16)
    for p in range(A_HEADS // 2):
        ql = _dot(qa[:, 128 * p:128 * (p + 1)], wuk_ref[p])
        qlat_ref[:, 256 * p:256 * (p + 1)] = (ql * HEAD_DIM ** -0.5).astype(BF16)
    c = _dot(xb, w_ref[:, 512:640])
    ms = jnp.mean(c * c, axis=-1, keepdims=True)
    ckv_ref[...] = (c * lax.rsqrt(ms + 1e-6) * g_ref[...]).astype(BF16)
    qidx_ref[...] = (_dot(xb, w_ref[:, 640:896]) * IDX_DIM ** -0.5).astype(BF16)
    kidx_ref[...] = _dot(xb, w_ref[:, 896:1024]).astype(BF16)
    widx_ref[...] = _dot(xb, w_ref[:, 1024:1152]) * IDX_HEADS ** -0.5
    qb_ref[...] = (_dot(xb, w_ref[:, 1152:1664]) * HEAD_DIM ** -0.5).astype(BF16)
    kb_ref[...] = _dot(xb, w_ref[:, 1664:2176]).astype(BF16)
    vb_ref[...] = _dot(xb, w_ref[:, 2176:2688]).astype(BF16)


def _proj_even(x2, w_in, kv_norm, w_uk, tm=512):
    n = x2.shape[0]
    cuts = np.cumsum([512, 128, 256, 64, 4, 512, 512, 512])[:-1].tolist()
    q_a, c_kv, q_idx, k_idx, w_idx, q_b, k_b, v_b = jnp.split(w_in, cuts, axis=1)
    w_idx = jnp.pad(w_idx, ((0, 0), (0, LANES - IDX_HEADS)))
    w = jnp.concatenate([q_a, c_kv, q_idx, k_idx, k_idx, w_idx, q_b, k_b, v_b], axis=1).astype(BF16)
    z = jnp.zeros((HEAD_DIM, A_KV_RANK), F32)
    wuk = jnp.stack([jnp.block([[w_uk[2 * p], z], [z, w_uk[2 * p + 1]]]) for p in range(A_HEADS // 2)]).astype(BF16)
    g = kv_norm.reshape(1, A_KV_RANK)
    widths = [(1024, BF16), (128, BF16), (256, BF16), (128, BF16), (128, F32), (512, BF16), (512, BF16), (512, BF16)]
    return pl.pallas_call(
        _proj_even_kernel,
        grid=(n // tm,),
        in_specs=[pl.BlockSpec((tm, D_MODEL), lambda i: (i, 0)),
                  _const_spec(w.shape), _const_spec(wuk.shape), _const_spec(g.shape)],
        out_specs=[pl.BlockSpec((tm, wd), lambda i: (i, 0)) for wd, _ in widths],
        out_shape=[jax.ShapeDtypeStruct((n, wd), dt) for wd, dt in widths],
        compiler_params=_params(("parallel",)),
    )(x2, w, wuk, g)


def _dsa_kernel(qlat_ref, qidx_ref, widx_ref, kidx_ref, ckv_ref, wuv_ref, o_ref,
                key_ref, msk_ref, olat_ref, *, tq, topk):
    i = pl.program_id(1)
    t0 = i * tq
    n_sel = jnp.maximum(i + 1, topk // tq)
    lane = _iota((tq, LANES), 1)
    row = _iota((tq, LANES), 0)
    rel = (lane - row).astype(F32)

    qm, wb = [], []
    for h in range(IDX_HEADS):
        p, half = divmod(h, 2)
        qi = qidx_ref[0, :, 128 * p:128 * (p + 1)].astype(F32)
        qm.append(jnp.where((lane < IDX_DIM) == (half == 0), qi, 0.0).astype(BF16))
        wb.append(jnp.broadcast_to(widx_ref[0, :, h:h + 1], (tq, LANES)))

    def score_body(kt, carry):
        k0 = pl.multiple_of(kt * LANES, LANES)
        k2 = kidx_ref[0, pl.ds(k0, LANES), :]
        acc = jnp.zeros((tq, LANES), F32)
        for h in range(IDX_HEADS):
            acc = acc + wb[h] * jnp.maximum(_dot_nt(qm[h], k2), 0.0)
        acc = jnp.where(k0 + lane <= t0 + row, acc, NEG)
        bits = lax.bitcast_convert_type(acc, I32)
        key_ref[kt] = bits ^ ((bits >> 31) & 0x7FFFFFFF)
        return carry

    lax.fori_loop(0, n_sel, score_body, 0)

    def bit_body(bi, carry):
        cur, cfail = carry
        cand = cur + lax.shift_left(jnp.int32(1), 31 - bi)

        def cnt_body(kt, acc):
            return acc + jnp.where(key_ref[kt] >= cand, 1.0, 0.0)

        acc = lax.fori_loop(0, n_sel, cnt_body, jnp.zeros((tq, LANES), F32))
        cnt = jnp.sum(acc, axis=-1, keepdims=True)
        ok = cnt >= topk
        return jnp.where(ok, cand, cur), jnp.where(ok, cfail, cnt)

    thr, cnt_gt = lax.fori_loop(
        0, 32, bit_body,
        (jnp.full((tq, LANES), -2 ** 31, I32), jnp.zeros((tq, LANES), F32)))
    need = topk - cnt_gt

    tri = jnp.where((_iota((LANES, 2 * LANES), 0) <= _iota((LANES, 2 * LANES), 1)), 1.0, 0.0).astype(BF16)

    def mask_body(kt, seen):
        kk = key_ref[kt]
        eq = kk == thr
        pc = _dot(jnp.where(eq, 1.0, 0.0).astype(BF16), tri)
        take = eq & (pc[:, :LANES] + seen <= need)
        sel = ((kk > thr) | take) & (kt * LANES + lane <= t0 + row)
        msk_ref[kt] = jnp.where(sel, 0.0, NEG)
        return seen + pc[:, LANES:]

    lax.fori_loop(0, i + 1, mask_body, jnp.zeros((tq, LANES), F32))

    for h in range(A_HEADS):
        slope = 2.0 ** (-8.0 * (h + 1) / A_HEADS)
        q_h = qlat_ref[0, :, 128 * h:128 * (h + 1)]

        def att_body(kt, carry, q_h=q_h, slope=slope):
            m, l, acc = carry
            k0 = pl.multiple_of(kt * LANES, LANES)
            c = ckv_ref[0, pl.ds(k0, LANES), :]
            s = _dot_nt(q_h, c) + slope * (rel + (k0 - t0).astype(F32)) + msk_ref[kt]
            m_new = jnp.maximum(m, jnp.max(s, axis=-1, keepdims=True))
            alpha = jnp.exp(m - m_new)
            p = jnp.exp(s - m_new)
            l = alpha * l + jnp.sum(p, axis=-1, keepdims=True)
            acc = alpha * acc + _dot(p.astype(BF16), c)
            return m_new, l, acc

        m, l, acc = lax.fori_loop(
            0, i + 1, att_body,
            (jnp.full((tq, 1), NEG, F32), jnp.zeros((tq, 1), F32), jnp.zeros((tq, A_KV_RANK), F32)))
        olat_ref[:, 128 * h:128 * (h + 1)] = acc / l

    for p in range(A_HEADS // 2):
        o = _dot(olat_ref[:, 256 * p:256 * (p + 1)].astype(BF16), wuv_ref[p])
        o_ref[0, :, 128 * p:128 * (p + 1)] = o.astype(BF16)


def _dsa_attention(qlat, qidx, widx, kidx, ckv, w_uv, tq=128):
    b, s, _ = qlat.shape
    topk = min(IDX_TOPK_MAX, s // 4)
    assert topk % tq == 0 and s % tq == 0
    z = jnp.zeros((A_KV_RANK, HEAD_DIM), F32)
    wuv = jnp.stack([jnp.block([[w_uv[2 * p], z], [z, w_uv[2 * p + 1]]]) for p in range(A_HEADS // 2)]).astype(BF16)
    nt = s // LANES
    return pl.pallas_call(
        functools.partial(_dsa_kernel, tq=tq, topk=topk),
        grid=(b, s // tq),
        in_specs=[pl.BlockSpec((1, tq, A_HEADS * A_KV_RANK), lambda bi, i: (bi, i, 0)),
                  pl.BlockSpec((1, tq, IDX_HEADS * IDX_DIM), lambda bi, i: (bi, i, 0)),
                  pl.BlockSpec((1, tq, LANES), lambda bi, i: (bi, i, 0)),
                  pl.BlockSpec((1, s, LANES), lambda bi, i: (bi, 0, 0)),
                  pl.BlockSpec((1, s, A_KV_RANK), lambda bi, i: (bi, 0, 0)),
                  _const_spec(wuv.shape)],
        out_specs=pl.BlockSpec((1, tq, A_HEADS * HEAD_DIM), lambda bi, i: (bi, i, 0)),
        out_shape=jax.ShapeDtypeStruct((b, s, A_HEADS * HEAD_DIM), BF16),
        scratch_shapes=[pltpu.VMEM((nt, tq, LANES), I32),
                        pltpu.VMEM((nt, tq, LANES), F32),
                        pltpu.VMEM((tq, A_HEADS * A_KV_RANK), F32)],
        compiler_params=_params(("parallel", "arbitrary")),
    )(qlat, qidx, widx, kidx, ckv, wuv)


def _stick_kernel(q_ref, k_ref, v_ref, tri_ref, o_ref, *, tq):
    i = pl.program_id(2)
    t0 = i * tq
    lane = _iota((tq, LANES), 1)
    row = _iota((tq, LANES), 0)
    q = q_ref[0].astype(F32)
    tri = tri_ref[...]
    outs = []
    for half in range(2):
        qm = jnp.where((lane < HEAD_DIM) == (half == 0), q, 0.0).astype(BF16)

        def body(j, carry, qm=qm):
            c, acc = carry
            k0 = pl.multiple_of((i - j) * LANES, LANES)
            kk = k_ref[0, pl.ds(k0, LANES), :]
            vv = v_ref[0, pl.ds(k0, LANES), :]
            z = _dot_nt(qm, kk)
            sp = jnp.log1p(jnp.exp(-jnp.abs(z)))
            log_beta = jnp.minimum(z, 0.0) - sp
            strict = k0 + lane < t0 + row
            log_1m = jnp.where(strict, -jnp.maximum(z, 0.0) - sp, 0.0)
            hi = log_1m.astype(BF16)
            lo = (log_1m - hi.astype(F32)).astype(BF16)
            r = _dot(jnp.concatenate([hi, lo], axis=1), tri)
            a = jnp.where(strict, jnp.exp(log_beta + r[:, :LANES] + c), 0.0)
            return c + r[:, LANES:], acc + _dot(a.astype(BF16), vv)

        _, acc = lax.fori_loop(0, i + 1, body,
                               (jnp.zeros((tq, LANES), F32), jnp.zeros((tq, LANES), F32)))
        outs.append(acc)
    o_ref[0] = jnp.where(lane < HEAD_DIM, outs[0], outs[1]).astype(BF16)


def _stick_attention(q, k, v, tq=128):
    b, s, w = q.shape
    jj = np.arange(2 * LANES)[:, None] % LANES
    ss = np.arange(2 * LANES)[None, :]
    tri = jnp.asarray(((ss >= LANES) | (jj > ss)).astype(np.float32), BF16)
    return pl.pallas_call(
        functools.partial(_stick_kernel, tq=tq),
        grid=(b, w // LANES, s // tq),
        in_specs=[pl.BlockSpec((1, tq, LANES), lambda bi, p, i: (bi, i, p)),
                  pl.BlockSpec((1, s, LANES), lambda bi, p, i: (bi, 0, p)),
                  pl.BlockSpec((1, s, LANES), lambda bi, p, i: (bi, 0, p)),
                  _const_spec(tri.shape)],
        out_specs=pl.BlockSpec((1, tq, LANES), lambda bi, p, i: (bi, i, p)),
        out_shape=jax.ShapeDtypeStruct((b, s, w), BF16),
        compiler_params=_params(("parallel", "parallel", "arbitrary")),
    )(q, k, v, tri)


def _outproj_ln_kernel(*refs, n_in):
    o_refs, w_refs = refs[:n_in], refs[n_in:2 * n_in]
    x_ref, g_ref, b_ref, out_ref = refs[2 * n_in:]
    m = _dot(o_refs[0][...], w_refs[0][...])
    for o_ref, w_ref in zip(o_refs[1:], w_refs[1:]):
        m = m + _dot(o_ref[...], w_ref[...])
    out_ref[...] = _layer_norm(DEEPNORM_ALPHA * x_ref[...] + m, g_ref[...], b_ref[...])


def _outproj_ln(os_, ws, x2, g, b, tm=512):
    n = x2.shape[0]
    n_in = len(os_)
    return pl.pallas_call(
        functools.partial(_outproj_ln_kernel, n_in=n_in),
        grid=(n // tm,),
        in_specs=([pl.BlockSpec((tm, o.shape[1]), lambda i: (i, 0)) for o in os_]
                  + [_const_spec(w.shape) for w in ws]
                  + [pl.BlockSpec((tm, D_MODEL), lambda i: (i, 0)),
                     _const_spec((1, D_MODEL)), _const_spec((1, D_MODEL))]),
        out_specs=pl.BlockSpec((tm, D_MODEL), lambda i: (i, 0)),
        out_shape=jax.ShapeDtypeStruct((n, D_MODEL), F32),
        compiler_params=_params(("parallel",)),
    )(*os_, *ws, x2, g.reshape(1, D_MODEL), b.reshape(1, D_MODEL))


def _ffn_ln_kernel(x_ref, wg_ref, wu_ref, wd_ref, g_ref, b_ref, out_ref, acc_ref, *, fc):
    x = x_ref[...]
    xb = x.astype(BF16)
    for c in range(D_FF // fc):
        gate = _dot(xb, wg_ref[:, c * fc:(c + 1) * fc])
        up = _dot(xb, wu_ref[:, c * fc:(c + 1) * fc])
        act = (gate * jax.nn.sigmoid(gate) * up).astype(BF16)
        d = _dot(act, wd_ref[c * fc:(c + 1) * fc, :])
        if c == 0:
            acc_ref[...] = d
        else:
            acc_ref[...] += d
    out_ref[...] = _layer_norm(DEEPNORM_ALPHA * x + acc_ref[...], g_ref[...], b_ref[...])


def _ffn_ln(x2, wg, wu, wd, g, b, tm=512, fc=256):
    n = x2.shape[0]
    single = dict(pipeline_mode=pl.Buffered(1))
    return pl.pallas_call(
        functools.partial(_ffn_ln_kernel, fc=fc),
        grid=(n // tm,),
        in_specs=[pl.BlockSpec((tm, D_MODEL), lambda i: (i, 0)),
                  pl.BlockSpec((D_MODEL, D_FF), lambda i: (0, 0), **single),
                  pl.BlockSpec((D_MODEL, D_FF), lambda i: (0, 0), **single),
                  pl.BlockSpec((D_FF, D_MODEL), lambda i: (0, 0), **single),
                  _const_spec((1, D_MODEL)), _const_spec((1, D_MODEL))],
        out_specs=pl.BlockSpec((tm, D_MODEL), lambda i: (i, 0)),
        out_shape=jax.ShapeDtypeStruct((n, D_MODEL), F32),
        scratch_shapes=[pltpu.VMEM((tm, D_MODEL), F32)],
        compiler_params=_params(("parallel",)),
    )(x2, wg.astype(BF16), wu.astype(BF16), wd.astype(BF16), g.reshape(1, D_MODEL), b.reshape(1, D_MODEL))


def _proj_odd_kernel(x_ref, w_ref, q_ref, k_ref, v_ref, sel_ref, kme_ref, *, nb):
    step = pl.program_id(0)
    n = step % nb
    nh = C_HEADS
    wq = nh * HEAD_DIM

    @pl.when(step == 0)
    def _():
        kme_ref[...] = jnp.zeros_like(kme_ref)

    xb = x_ref[...].astype(BF16)
    qf = _dot(xb, w_ref[:, 0:wq])
    kf = _dot(xb, w_ref[:, wq:2 * wq])
    q_ref[...] = (qf * HEAD_DIM ** -0.5).astype(BF16)
    k_ref[...] = kf.astype(BF16)
    v_ref[...] = _dot(xb, w_ref[:, 2 * wq:3 * wq]).astype(BF16)

    gate = _dot_nt(kme_ref[...], qf, precision=lax.Precision.HIGHEST)
    blk = _iota(gate.shape, 0) // nh
    gate = jnp.where(blk < n, gate, NEG)
    gs = [gate[nh * m:nh * (m + 1), :] for m in range(nb)]
    biases = []
    for a in range(nb):
        rank = jnp.zeros_like(gs[a])
        for m in range(nb):
            if m != a:
                beats = (gs[m] >= gs[a]) if m < a else (gs[m] > gs[a])
                rank = rank + jnp.where(beats, 1.0, 0.0)
        biases.append(jnp.where(rank < MOBA_TOPK, 0.0, NEG))
    bias = jnp.where(blk < n, jnp.concatenate(biases, axis=0), NEG)
    sel_ref[...] = bias.T.astype(BF16)

    kmean = jnp.mean(kf, axis=0, keepdims=True)
    head = _iota(kmean.shape, 1) // HEAD_DIM
    for h in range(nh):
        kme_ref[pl.ds(n * nh + h, 1), :] = jnp.where(head == h, kmean, 0.0)


def _proj_odd(x2, w_in, nb):
    n = x2.shape[0]
    p = MOBA_BLOCK
    wq = C_HEADS * HEAD_DIM
    w = w_in.astype(BF16)
    return pl.pallas_call(
        functools.partial(_proj_odd_kernel, nb=nb),
        grid=(n // p,),
        in_specs=[pl.BlockSpec((p, D_MODEL), lambda i: (i, 0)), _const_spec(w.shape)],
        out_specs=[pl.BlockSpec((p, wq), lambda i: (i, 0))] * 3 + [pl.BlockSpec((p, nb * C_HEADS), lambda i: (i, 0))],
        out_shape=[jax.ShapeDtypeStruct((n, wq), BF16)] * 3 + [jax.ShapeDtypeStruct((n, nb * C_HEADS), BF16)],
        scratch_shapes=[pltpu.VMEM((nb * C_HEADS, wq), F32)],
        compiler_params=_params(("arbitrary",)),
    )(x2, w)


def _moba_kernel(slopes_ref, q_ref, k_ref, v_ref, sel_ref, o_ref, *, nb):
    pr = pl.program_id(1)
    i = pl.program_id(2)
    p = MOBA_BLOCK
    lane = _iota((p, p), 1)
    row = _iota((p, p), 0)
    rel = (lane - row).astype(F32)
    lane_q = _iota((p, LANES), 1)
    q = q_ref[0].astype(F32)
    selb = sel_ref[0]
    r0 = pl.multiple_of(i * p, p)
    k_own = k_ref[0, pl.ds(r0, p), :]
    v_own = v_ref[0, pl.ds(r0, p), :]
    outs = []
    for half in range(2):
        h = 2 * pr + half
        slope = slopes_ref[h]
        qm = jnp.where((lane_q < HEAD_DIM) == (half == 0), q, 0.0).astype(BF16)
        bias = slope * rel

        s = jnp.where(lane <= row, _dot_nt(qm, k_own) + bias, NEG)
        m = jnp.max(s, axis=-1, keepdims=True)
        pexp = jnp.exp(s - m)
        l = jnp.sum(pexp, axis=-1, keepdims=True)
        acc = _dot(pexp.astype(BF16), v_own)

        def body(n, carry, qm=qm, bias=bias, slope=slope, h=h):
            m, l, acc = carry
            k0 = pl.multiple_of(n * p, p)
            kk = k_ref[0, pl.ds(k0, p), :]
            vv = v_ref[0, pl.ds(k0, p), :]
            onehot = jnp.where(_iota((nb * C_HEADS, p), 0) == n * C_HEADS + h, 1.0, 0.0).astype(BF16)
            s = _dot_nt(qm, kk) + (bias + slope * ((n - i) * p).astype(F32)) + _dot(selb, onehot)
            m_new = jnp.maximum(m, jnp.max(s, axis=-1, keepdims=True))
            alpha = jnp.exp(m - m_new)
            pexp = jnp.exp(s - m_new)
            l = alpha * l + jnp.sum(pexp, axis=-1, keepdims=True)
            acc = alpha * acc + _dot(pexp.astype(BF16), vv)
            return m_new, l, acc

        m, l, acc = lax.fori_loop(0, i, body, (m, l, acc))
        outs.append(acc / l)
    o_ref[0] = jnp.where(lane_q < HEAD_DIM, outs[0], outs[1]).astype(BF16)


def _moba_attention(q, k, v, sel):
    b, s, w = q.shape
    p = MOBA_BLOCK
    nb = s // p
    slopes = jnp.asarray(2.0 ** (-8.0 * np.arange(1, C_HEADS + 1) / C_HEADS), F32)
    return pl.pallas_call(
        functools.partial(_moba_kernel, nb=nb),
        grid=(b, w // LANES, nb),
        in_specs=[pl.BlockSpec(memory_space=pltpu.SMEM),
                  pl.BlockSpec((1, p, LANES), lambda bi, hp, i: (bi, i, hp)),
                  pl.BlockSpec((1, s, LANES), lambda bi, hp, i: (bi, 0, hp)),
                  pl.BlockSpec((1, s, LANES), lambda bi, hp, i: (bi, 0, hp)),
                  pl.BlockSpec((1, p, nb * C_HEADS), lambda bi, hp, i: (bi, i, 0))],
        out_specs=pl.BlockSpec((1, p, LANES), lambda bi, hp, i: (bi, i, hp)),
        out_shape=jax.ShapeDtypeStruct((b, s, w), BF16),
        compiler_params=_params(("parallel", "parallel", "arbitrary")),
    )(slopes, q, k, v, sel)


def kernel(x, even_w_in, even_kv_norm, even_w_uk, even_w_uv, even_w_out, odd_w_in, odd_w_out,
           ln_mix_g, ln_mix_b, ffn_gate, ffn_up, ffn_down, ln_ffn_g, ln_ffn_b):
    b, s, d = x.shape
    assert d == D_MODEL and s % MOBA_BLOCK == 0 and s >= 4 * IDX_TOPK_MAX
    h = x.reshape(b * s, d)
    for i in range(DEPTH):
        j = i // 2
        if i % 2 == 0:
            qlat, ckv, qidx, kidx, widx, qb, kb, vb = _proj_even(h, even_w_in[j], even_kv_norm[j], even_w_uk[j])
            r3 = lambda t: t.reshape(b, s, t.shape[-1])
            o_a = _dsa_attention(r3(qlat), r3(qidx), r3(widx), r3(kidx), r3(ckv), even_w_uv[j])
            o_b = _stick_attention(r3(qb), r3(kb), r3(vb))
            w_out = even_w_out[j].astype(BF16)
            na = A_HEADS * HEAD_DIM
            outs = [o_a.reshape(b * s, -1), o_b.reshape(b * s, -1)]
            ws = [w_out[:na], w_out[na:]]
        else:
            q, k, v, sel = _proj_odd(h, odd_w_in[j], s // MOBA_BLOCK)
            r3 = lambda t: t.reshape(b, s, t.shape[-1])
            o_c = _moba_attention(r3(q), r3(k), r3(v), r3(sel))
            outs = [o_c.reshape(b * s, -1)]
            ws = [odd_w_out[j].astype(BF16)]
        h = _outproj_ln(outs, ws, h, ln_mix_g[i], ln_mix_b[i])
        h = _ffn_ln(h, ffn_gate[i], ffn_up[i], ffn_down[i], ln_ffn_g[i], ln_ffn_b[i])
    return h.reshape(b, s, d)
```

```python
import functools

import numpy as np
import jax
import jax.numpy as jnp
from jax import lax
from jax.experimental import pallas as pl
from jax.experimental.pallas import tpu as pltpu

D_MODEL = 1024
DEPTH = 2
HEAD_DIM = 64
A_HEADS = 8
A_KV_RANK = 128
IDX_HEADS = 4
IDX_DIM = 64
IDX_TOPK_MAX = 256
B_HEADS = 8
C_HEADS = 16
MOBA_BLOCK = 256
MOBA_TOPK = 3
D_FF = -(-8 * D_MODEL // (3 * 256)) * 256
DEEPNORM_ALPHA = (2 * DEPTH) ** 0.25
NEG = -1e30

LANES = 128
VMEM_LIMIT = 52 * 1024 * 1024

BF16 = jnp.bfloat16
F32 = jnp.float32
I32 = jnp.int32


def _dot(a, b):
    return jnp.dot(a, b, preferred_element_type=F32)


def _dot_nt(a, b, precision=None):
    return lax.dot_general(a, b, (((1,), (1,)), ((), ())), precision=precision,
                           preferred_element_type=F32)


def _iota(shape, dim):
    return lax.broadcasted_iota(I32, shape, dim)


def _params(sem, vmem=VMEM_LIMIT):
    return pltpu.CompilerParams(dimension_semantics=sem, vmem_limit_bytes=vmem)


def _const_spec(shape):
    nd = len(shape)
    return pl.BlockSpec(shape, lambda *_: (0,) * nd)


def _layer_norm(y, g, b):
    mu = jnp.mean(y, axis=-1, keepdims=True)
    d = y - mu
    var = jnp.mean(d * d, axis=-1, keepdims=True)
    return d * lax.rsqrt(var + 1e-5) * g + b


def _proj_even_kernel(x_ref, w_ref, wuk_ref, g_ref,
                      qlat_ref, ckv_ref, qidx_ref, kidx_ref, widx_ref, qb_ref, kb_ref, vb_ref):
    xb = x_ref[...].astype(BF16)
    qa = _dot(xb, w_ref[:, 0:512]).astype(BF16)
    for p in range(A_HEADS // 2):
        ql = _dot(qa[:, 128 * p:128 * (p + 1)], wuk_ref[p])
        qlat_ref[:, 256 * p:256 * (p + 1)] = (ql * HEAD_DIM ** -0.5).astype(BF16)
    c = _dot(xb, w_ref[:, 512:640])
    ms = jnp.mean(c * c, axis=-1, keepdims=True)
    ckv_ref[...] = (c * lax.rsqrt(ms + 1e-6) * g_ref[...]).astype(BF16)
    qidx_ref[...] = (_dot(xb, w_ref[:, 640:896]) * IDX_DIM ** -0.5).astype(BF16)
    kidx_ref[...] = _dot(xb, w_ref[:, 896:1024]).astype(BF16)
    widx_ref[...] = _dot(xb, w_ref[:, 1024:1152]) * IDX_HEADS ** -0.5
    qb_ref[...] = (_dot(xb, w_ref[:, 1152:1664]) * HEAD_DIM ** -0.5).astype(BF16)
    kb_ref[...] = _dot(xb, w_ref[:, 1664:2176]).astype(BF16)
    vb_ref[...] = _dot(xb, w_ref[:, 2176:2688]).astype(BF16)


def _proj_even(x2, w_in, kv_norm, w_uk, tm=512):
    n = x2.shape[0]
    cuts = np.cumsum([512, 128, 256, 64, 4, 512, 512, 512])[:-1].tolist()
    q_a, c_kv, q_idx, k_idx, w_idx, q_b, k_b, v_b = jnp.split(w_in, cuts, axis=1)
    w_idx = jnp.pad(w_idx, ((0, 0), (0, LANES - IDX_HEADS)))
    w = jnp.concatenate([q_a, c_kv, q_idx, k_idx, k_idx, w_idx, q_b, k_b, v_b], axis=1).astype(BF16)
    z = jnp.zeros((HEAD_DIM, A_KV_RANK), F32)
    wuk = jnp.stack([jnp.block([[w_uk[2 * p], z], [z, w_uk[2 * p + 1]]]) for p in range(A_HEADS // 2)]).astype(BF16)
    g = kv_norm.reshape(1, A_KV_RANK)
    widths = [(1024, BF16), (128, BF16), (256, BF16), (128, BF16), (128, F32), (512, BF16), (512, BF16), (512, BF16)]
    return pl.pallas_call(
        _proj_even_kernel,
        grid=(n // tm,),
        in_specs=[pl.BlockSpec((tm, D_MODEL), lambda i: (i, 0)),
                  _const_spec(w.shape), _const_spec(wuk.shape), _const_spec(g.shape)],
        out_specs=[pl.BlockSpec((tm, wd), lambda i: (i, 0)) for wd, _ in widths],
        out_shape=[jax.ShapeDtypeStruct((n, wd), dt) for wd, dt in widths],
        compiler_params=_params(("parallel",)),
    )(x2, w, wuk, g)


def _dsa_kernel(qlat_ref, qidx_ref, widx_ref, kidx_ref, ckv_ref, kfeat_ref, wuv_ref, o_ref,
                key_ref, msk_ref, ckv2_ref, logit_ref, mrun_ref, acc_ref, olat_ref, *, tq, topk):
    i = pl.program_id(1)
    t0 = i * tq
    tk = tq
    hq = tq // 2
    nt = key_ref.shape[0]
    n_pair = (i + 2) // 2
    int_min = -2 ** 31

    @pl.when(i == 0)
    def _():
        ckv2_ref[:, 0:LANES] = ckv_ref[0]
        ckv2_ref[:, LANES:2 * LANES] = kfeat_ref[...]

    @pl.when(i + 1 < nt)
    def _():
        key_ref[i + 1] = jnp.full((tq, tk), int_min, I32)
        msk_ref[i + 1] = jnp.full((tq, tk), NEG, F32)

    rel = _iota((tq, tk), 1) - _iota((tq, tk), 0)
    lane_q = _iota((tq, LANES), 1)

    qm, wb = [], []
    for h in range(IDX_HEADS):
        p, half = divmod(h, 2)
        qi = qidx_ref[0, :, 128 * p:128 * (p + 1)].astype(F32)
        qm.append(jnp.where((lane_q < IDX_DIM) == (half == 0), qi, 0.0).astype(BF16))
        wb.append(jnp.broadcast_to(widx_ref[0, :, h:h + 1], (tq, tk)))

    def score_body(kt, carry):
        k0 = pl.multiple_of(kt * tk, tk)
        k2 = kidx_ref[0, pl.ds(k0, tk), :]
        acc = wb[0] * jnp.maximum(_dot_nt(qm[0], k2), 0.0)
        for h in range(1, IDX_HEADS):
            acc = acc + wb[h] * jnp.maximum(_dot_nt(qm[h], k2), 0.0)
        acc = jnp.where(rel <= t0 - k0, acc, NEG)
        bits = lax.bitcast_convert_type(acc, I32)
        key_ref[kt] = bits ^ ((bits >> 31) & 0x7FFFFFFF)
        return carry

    lax.fori_loop(0, i + 1, score_body, 0)

    def bit_body(bi, carry):
        cur, cfail = carry
        cand = cur + lax.shift_left(jnp.int32(1), 31 - bi)
        accs = []
        for half in range(2):
            ch = cand[half * hq:(half + 1) * hq]

            def cnt_body(kp, acc, ch=ch, half=half):
                for u in range(2):
                    kk = key_ref[2 * kp + u, half * hq:(half + 1) * hq, :]
                    acc = (acc + jnp.where(kk[:, :LANES] >= ch, 1.0, 0.0)
                           + jnp.where(kk[:, LANES:] >= ch, 1.0, 0.0))
                return acc

            accs.append(lax.fori_loop(0, n_pair, cnt_body, jnp.zeros((hq, LANES), F32)))
        cnt = jnp.sum(jnp.concatenate(accs, axis=0), axis=-1, keepdims=True)
        ok = cnt >= topk
        return jnp.where(ok, cand, cur), jnp.where(ok, cfail, cnt)

    thr, cnt_gt = lax.fori_loop(
        0, 32, bit_body,
        (jnp.full((tq, LANES), int_min, I32), jnp.zeros((tq, LANES), F32)))
    need = topk - cnt_gt
    thr2 = jnp.concatenate([thr, thr], axis=1)
    need2 = jnp.concatenate([need, need], axis=1)

    tri = jnp.where(_iota((tk, 2 * tk), 0) <= _iota((tk, 2 * tk), 1), 1.0, 0.0).astype(BF16)

    def mask_body(kt, seen):
        kk = key_ref[kt]
        eq = kk == thr2
        pc = _dot(jnp.where(eq, 1.0, 0.0).astype(BF16), tri)
        take = eq & (pc[:, :tk] + jnp.concatenate([seen, seen], axis=1) <= need2)
        sel = ((kk > thr2) | take) & (rel <= t0 - kt * tk)
        msk_ref[kt] = jnp.where(sel, 0.0, NEG)
        return seen + pc[:, tk:tk + LANES]

    lax.fori_loop(0, i + 1, mask_body, jnp.zeros((tq, LANES), F32))

    lane_h = _iota((hq, LANES), 1)
    for half in range(2):
        r0, r1 = half * hq, (half + 1) * hq
        t0h = (t0 + r0).astype(F32)
        parts = []
        for h in range(A_HEADS):
            slope = 2.0 ** (-8.0 * (h + 1) / A_HEADS)
            qf = jnp.where(lane_h < 2, slope, jnp.where(lane_h == 2, -slope * t0h, 0.0)).astype(BF16)
            parts.append(jnp.concatenate([qlat_ref[0, r0:r1, 128 * h:128 * (h + 1)], qf], axis=1))
        lhs = jnp.concatenate(parts, axis=0)
        mrun_ref[...] = jnp.full(mrun_ref.shape, NEG, F32)

        def pass1(kt, carry, lhs=lhs, r0=r0, r1=r1):
            k0 = pl.multiple_of(kt * tk, tk)
            s = _dot_nt(lhs, ckv2_ref[pl.ds(k0, tk), :])
            mk = msk_ref[kt, r0:r1, :]
            for h in range(A_HEADS):
                sh = s[h * hq:(h + 1) * hq] + mk
                logit_ref[kt, h * hq:(h + 1) * hq, :] = sh
                mrun_ref[h * hq:(h + 1) * hq, :] = jnp.maximum(
                    mrun_ref[h * hq:(h + 1) * hq, :], jnp.maximum(sh[:, :LANES], sh[:, LANES:]))
            return carry

        lax.fori_loop(0, 2 * n_pair, pass1, 0)
        m = jnp.max(mrun_ref[...], axis=-1, keepdims=True)
        acc_ref[...] = jnp.zeros(acc_ref.shape, F32)

        def pass2(kp, carry, m=m):
            p0 = jnp.exp(logit_ref[2 * kp] - m).astype(BF16)
            p1 = jnp.exp(logit_ref[2 * kp + 1] - m).astype(BF16)
            k0 = pl.multiple_of(kp * 2 * tk, 2 * tk)
            acc_ref[...] += _dot(jnp.concatenate([p0, p1], axis=1), ckv2_ref[pl.ds(k0, 2 * tk), :])
            return carry

        lax.fori_loop(0, n_pair, pass2, 0)
        acc = acc_ref[...]
        o = acc[:, :LANES] / acc[:, LANES + 2:LANES + 3]
        for h in range(A_HEADS):
            olat_ref[r0:r1, 128 * h:128 * (h + 1)] = o[h * hq:(h + 1) * hq]

    for p in range(A_HEADS // 2):
        o = _dot(olat_ref[:, 256 * p:256 * (p + 1)].astype(BF16), wuv_ref[p])
        o_ref[0, :, 128 * p:128 * (p + 1)] = o.astype(BF16)


def _dsa_attention(qlat, qidx, widx, kidx, ckv, w_uv, tq=256):
    b, s, _ = qlat.shape
    topk = min(IDX_TOPK_MAX, s // 4)
    assert topk <= tq and s % (2 * tq) == 0
    z = jnp.zeros((A_KV_RANK, HEAD_DIM), F32)
    wuv = jnp.stack([jnp.block([[w_uv[2 * p], z], [z, w_uv[2 * p + 1]]]) for p in range(A_HEADS // 2)]).astype(BF16)
    pos = np.arange(s)
    kfeat = np.zeros((s, LANES), np.float32)
    kfeat[:, 0], kfeat[:, 1], kfeat[:, 2] = 256 * (pos // 256), pos % 256, 1.0
    kfeat = jnp.asarray(kfeat, BF16)
    nt = s // tq
    rows = A_HEADS * tq // 2
    return pl.pallas_call(
        functools.partial(_dsa_kernel, tq=tq, topk=topk),
        grid=(b, s // tq),
        in_specs=[pl.BlockSpec((1, tq, A_HEADS * A_KV_RANK), lambda bi, i: (bi, i, 0)),
                  pl.BlockSpec((1, tq, IDX_HEADS * IDX_DIM), lambda bi, i: (bi, i, 0)),
                  pl.BlockSpec((1, tq, LANES), lambda bi, i: (bi, i, 0)),
                  pl.BlockSpec((1, s, LANES), lambda bi, i: (bi, 0, 0)),
                  pl.BlockSpec((1, s, A_KV_RANK), lambda bi, i: (bi, 0, 0)),
                  _const_spec(kfeat.shape), _const_spec(wuv.shape)],
        out_specs=pl.BlockSpec((1, tq, A_HEADS * HEAD_DIM), lambda bi, i: (bi, i, 0)),
        out_shape=jax.ShapeDtypeStruct((b, s, A_HEADS * HEAD_DIM), BF16),
        scratch_shapes=[pltpu.VMEM((nt, tq, tq), I32),
                        pltpu.VMEM((nt, tq, tq), F32),
                        pltpu.VMEM((s, 2 * LANES), BF16),
                        pltpu.VMEM((nt, rows, tq), F32),
                        pltpu.VMEM((rows, LANES), F32),
                        pltpu.VMEM((rows, 2 * LANES), F32),
                        pltpu.VMEM((tq, A_HEADS * A_KV_RANK), F32)],
        compiler_params=_params(("parallel", "arbitrary")),
    )(qlat, qidx, widx, kidx, ckv, kfeat, wuv)


def _stick_kernel(q_ref, k_ref, v_ref, tri_ref, o_ref, kbd_ref, vbd_ref, c_ref, acc_ref, *, tq):
    i = pl.program_id(2)
    t0 = i * tq
    nt = kbd_ref.shape[0]
    sub = tq // LANES

    @pl.when(i == 0)
    def _():
        lo = _iota((LANES, LANES), 1) < HEAD_DIM

        def build(kt, carry):
            r0 = pl.multiple_of(kt * LANES, LANES)
            kk = k_ref[0, pl.ds(r0, LANES), :].astype(F32)
            vv = v_ref[0, pl.ds(r0, LANES), :].astype(F32)
            kbd_ref[kt, 0:LANES, :] = jnp.where(lo, kk, 0.0).astype(BF16)
            kbd_ref[kt, LANES:2 * LANES, :] = jnp.where(lo, 0.0, kk).astype(BF16)
            vbd_ref[kt, 0:LANES, :] = jnp.where(lo, vv, 0.0).astype(BF16)
            vbd_ref[kt, LANES:2 * LANES, :] = jnp.where(lo, 0.0, vv).astype(BF16)
            return carry

        lax.fori_loop(0, nt, build, 0)

    q = q_ref[0]
    tri = tri_ref[...]
    c_ref[...] = jnp.zeros(c_ref.shape, F32)
    acc_ref[...] = jnp.zeros(acc_ref.shape, F32)

    def tile(kt, masked):
        z = _dot_nt(q, kbd_ref[kt])
        sp = jnp.log(1.0 + jnp.exp(-jnp.abs(z)))
        t = jnp.maximum(z, 0.0) + sp
        log_beta = z - t
        if masked:
            col = _iota((tq, 2 * LANES), 1) & (LANES - 1)
            strict = kt * LANES + col < t0 + _iota((tq, 2 * LANES), 0)
            t = jnp.where(strict, t, 0.0)
        hi = t.astype(BF16)
        lo = (t - hi.astype(F32)).astype(BF16)
        lhs = jnp.concatenate([jnp.concatenate([hi[:, :LANES], lo[:, :LANES]], axis=1),
                               jnp.concatenate([hi[:, LANES:], lo[:, LANES:]], axis=1)], axis=0)
        r = _dot(lhs, tri)
        rem = jnp.concatenate([r[:tq, :LANES], r[tq:, :LANES]], axis=1) + c_ref[...]
        a = jnp.exp(log_beta + rem)
        if masked:
            a = jnp.where(strict, a, 0.0)
        acc_ref[...] += _dot(a.astype(BF16), vbd_ref[kt])
        c_ref[...] += jnp.concatenate([r[:tq, LANES:], r[tq:, LANES:]], axis=1)

    for d in range(sub):
        tile((i + 1) * sub - 1 - d, True)

    def body(j, carry):
        tile(i * sub - 1 - j, False)
        return carry

    lax.fori_loop(0, i * sub, body, 0)
    o_ref[0] = acc_ref[...].astype(BF16)


def _stick_attention(q, k, v, tq=256):
    b, s, w = q.shape
    jj = np.arange(2 * LANES)[:, None] % LANES
    ss = np.arange(2 * LANES)[None, :]
    tri = jnp.asarray(-((ss >= LANES) | (jj > ss)).astype(np.float32), BF16)
    nt = s // LANES
    return pl.pallas_call(
        functools.partial(_stick_kernel, tq=tq),
        grid=(b, w // LANES, s // tq),
        in_specs=[pl.BlockSpec((1, tq, LANES), lambda bi, p, i: (bi, i, p)),
                  pl.BlockSpec((1, s, LANES), lambda bi, p, i: (bi, 0, p)),
                  pl.BlockSpec((1, s, LANES), lambda bi, p, i: (bi, 0, p)),
                  _const_spec(tri.shape)],
        out_specs=pl.BlockSpec((1, tq, LANES), lambda bi, p, i: (bi, i, p)),
        out_shape=jax.ShapeDtypeStruct((b, s, w), BF16),
        scratch_shapes=[pltpu.VMEM((nt, 2 * LANES, LANES), BF16),
                        pltpu.VMEM((nt, 2 * LANES, LANES), BF16),
                        pltpu.VMEM((tq, 2 * LANES), F32),
                        pltpu.VMEM((tq, LANES), F32)],
        compiler_params=_params(("parallel", "parallel", "arbitrary")),
    )(q, k, v, tri)


def _outproj_ln_kernel(*refs, n_in):
    o_refs, w_refs = refs[:n_in], refs[n_in:2 * n_in]
    x_ref, g_ref, b_ref, out_ref = refs[2 * n_in:]
    m = _dot(o_refs[0][...], w_refs[0][...])
    for o_ref, w_ref in zip(o_refs[1:], w_refs[1:]):
        m = m + _dot(o_ref[...], w_ref[...])
    out_ref[...] = _layer_norm(DEEPNORM_ALPHA * x_ref[...] + m, g_ref[...], b_ref[...])


def _outproj_ln(os_, ws, x2, g, b, tm=512):
    n = x2.shape[0]
    n_in = len(os_)
    return pl.pallas_call(
        functools.partial(_outproj_ln_kernel, n_in=n_in),
        grid=(n // tm,),
        in_specs=([pl.BlockSpec((tm, o.shape[1]), lambda i: (i, 0)) for o in os_]
                  + [_const_spec(w.shape) for w in ws]
                  + [pl.BlockSpec((tm, D_MODEL), lambda i: (i, 0)),
                     _const_spec((1, D_MODEL)), _const_spec((1, D_MODEL))]),
        out_specs=pl.BlockSpec((tm, D_MODEL), lambda i: (i, 0)),
        out_shape=jax.ShapeDtypeStruct((n, D_MODEL), F32),
        compiler_params=_params(("parallel",)),
    )(*os_, *ws, x2, g.reshape(1, D_MODEL), b.reshape(1, D_MODEL))


def _ffn_ln_kernel(x_ref, wg_ref, wu_ref, wd_ref, g_ref, b_ref, out_ref, acc_ref, *, fc):
    x = x_ref[...]
    xb = x.astype(BF16)
    for c in range(D_FF // fc):
        gate = _dot(xb, wg_ref[:, c * fc:(c + 1) * fc])
        up = _dot(xb, wu_ref[:, c * fc:(c + 1) * fc])
        act = (gate * jax.nn.sigmoid(gate) * up).astype(BF16)
        d = _dot(act, wd_ref[c * fc:(c + 1) * fc, :])
        if c == 0:
            acc_ref[...] = d
        else:
            acc_ref[...] += d
    out_ref[...] = _layer_norm(DEEPNORM_ALPHA * x + acc_ref[...], g_ref[...], b_ref[...])


def _ffn_ln(x2, wg, wu, wd, g, b, tm=512, fc=256):
    n = x2.shape[0]
    single = dict(pipeline_mode=pl.Buffered(1))
    return pl.pallas_call(
        functools.partial(_ffn_ln_kernel, fc=fc),
        grid=(n // tm,),
        in_specs=[pl.BlockSpec((tm, D_MODEL), lambda i: (i, 0)),
                  pl.BlockSpec((D_MODEL, D_FF), lambda i: (0, 0), **single),
                  pl.BlockSpec((D_MODEL, D_FF), lambda i: (0, 0), **single),
                  pl.BlockSpec((D_FF, D_MODEL), lambda i: (0, 0), **single),
                  _const_spec((1, D_MODEL)), _const_spec((1, D_MODEL))],
        out_specs=pl.BlockSpec((tm, D_MODEL), lambda i: (i, 0)),
        out_shape=jax.ShapeDtypeStruct((n, D_MODEL), F32),
        scratch_shapes=[pltpu.VMEM((tm, D_MODEL), F32)],
        compiler_params=_params(("parallel",)),
    )(x2, wg.astype(BF16), wu.astype(BF16), wd.astype(BF16), g.reshape(1, D_MODEL), b.reshape(1, D_MODEL))


MOBA_SEL_LANE = 0
MOBA_POS_LANE = 32


def _proj_odd_kernel(x_ref, w_ref, perm_ref, slope_ref, q_ref, k_ref, v_ref, qf_ref, kme_ref, *, nb):
    step = pl.program_id(0)
    n = step % nb
    nh = C_HEADS
    wq = nh * HEAD_DIM

    @pl.when(step == 0)
    def _():
        kme_ref[...] = jnp.zeros_like(kme_ref)

    xb = x_ref[...].astype(BF16)
    qf = _dot(xb, w_ref[:, 0:wq])
    kf = _dot(xb, w_ref[:, wq:2 * wq])
    q_ref[...] = (qf * HEAD_DIM ** -0.5).astype(BF16)
    k_ref[...] = kf.astype(BF16)
    v_ref[...] = _dot(xb, w_ref[:, 2 * wq:3 * wq]).astype(BF16)

    gate = _dot_nt(kme_ref[...], qf, precision=lax.Precision.HIGHEST)
    blk = _iota(gate.shape, 0) // nh
    gate = jnp.where(blk < n, gate, NEG)
    gs = [gate[nh * m:nh * (m + 1), :] for m in range(nb)]
    biases = []
    for a in range(nb):
        rank = jnp.zeros_like(gs[a])
        for m in range(nb):
            if m != a:
                beats = (gs[m] >= gs[a]) if m < a else (gs[m] > gs[a])
                rank = rank + jnp.where(beats, 1.0, 0.0)
        biases.append(jnp.where(rank < MOBA_TOPK, 0.0, NEG))
    bias = jnp.where(blk < n, jnp.concatenate(biases, axis=0), jnp.where(blk == n, 0.0, NEG))
    qf_ref[...] = (_dot(bias.T.astype(BF16), perm_ref[...]) + slope_ref[...]).astype(BF16)

    kmean = jnp.mean(kf, axis=0, keepdims=True)
    head = _iota(kmean.shape, 1) // HEAD_DIM
    for h in range(nh):
        kme_ref[pl.ds(n * nh + h, 1), :] = jnp.where(head == h, kmean, 0.0)


def _bf16_pieces(x):
    x = np.float32(x)
    out = []
    for _ in range(3):
        p = np.float32(np.asarray(x, dtype=BF16))
        out.append(p)
        x = np.float32(x - p)
    assert x == 0.0
    return out


def _moba_constants(nb, s):
    nh, p = C_HEADS, MOBA_BLOCK
    assert 2 * nb <= MOBA_POS_LANE and MOBA_POS_LANE + 12 <= LANES
    slopes = np.asarray(2.0 ** (-8.0 * np.arange(1, nh + 1) / nh), np.float32)
    perm = np.zeros((nb * nh, nh // 2 * LANES), np.float32)
    srow = np.zeros((1, nh // 2 * LANES), np.float32)
    for h in range(nh):
        base = (h // 2) * LANES
        for m in range(nb):
            perm[m * nh + h, base + MOBA_SEL_LANE + (h % 2) * nb + m] = 1.0
        pieces = _bf16_pieces(slopes[h])
        for c in range(6):
            srow[0, base + MOBA_POS_LANE + 6 * (h % 2) + c] = pieces[c % 3]
    pos = np.arange(s)
    kf = np.zeros((2, s, LANES), np.float32)
    for half in range(2):
        kf[half, pos, MOBA_SEL_LANE + half * nb + pos // p] = 1.0
        for c in range(3):
            kf[half, :, MOBA_POS_LANE + 6 * half + c] = pos % p
            kf[half, :, MOBA_POS_LANE + 6 * half + 3 + c] = p * (pos // p)
    vf = np.zeros((2, p, LANES), np.float32)
    vf[0, :, 0] = 1.0
    vf[1, :, 1] = 1.0
    return (jnp.asarray(perm, BF16), jnp.asarray(srow, F32), jnp.asarray(kf, BF16), jnp.asarray(vf, BF16))


def _proj_odd(x2, w_in, perm, srow, nb):
    n = x2.shape[0]
    p = MOBA_BLOCK
    wq = C_HEADS * HEAD_DIM
    w = w_in.astype(BF16)
    fw = perm.shape[1]
    return pl.pallas_call(
        functools.partial(_proj_odd_kernel, nb=nb),
        grid=(n // p,),
        in_specs=[pl.BlockSpec((p, D_MODEL), lambda i: (i, 0)), _const_spec(w.shape),
                  _const_spec(perm.shape), _const_spec(srow.shape)],
        out_specs=[pl.BlockSpec((p, wq), lambda i: (i, 0))] * 3 + [pl.BlockSpec((p, fw), lambda i: (i, 0))],
        out_shape=[jax.ShapeDtypeStruct((n, wq), BF16)] * 3 + [jax.ShapeDtypeStruct((n, fw), BF16)],
        scratch_shapes=[pltpu.VMEM((nb * C_HEADS, wq), F32)],
        compiler_params=_params(("arbitrary",)),
    )(x2, w, perm, srow)


def _moba_kernel(q_ref, qf_ref, k_ref, v_ref, kf_ref, vf_ref, o_ref, kbd_ref, vbd_ref, *, nb):
    i = pl.program_id(2)
    p = MOBA_BLOCK

    @pl.when(i == 0)
    def _():
        lo = _iota((p, LANES), 1) < HEAD_DIM

        def build(n, carry):
            r0 = pl.multiple_of(n * p, p)
            kk = k_ref[0, pl.ds(r0, p), :].astype(F32)
            vv = v_ref[0, pl.ds(r0, p), :].astype(F32)
            kbd_ref[n, 0:p, 0:LANES] = jnp.where(lo, kk, 0.0).astype(BF16)
            kbd_ref[n, 0:p, LANES:2 * LANES] = kf_ref[0, pl.ds(r0, p), :]
            kbd_ref[n, p:2 * p, 0:LANES] = jnp.where(lo, 0.0, kk).astype(BF16)
            kbd_ref[n, p:2 * p, LANES:2 * LANES] = kf_ref[1, pl.ds(r0, p), :]
            vbd_ref[n, 0:p, 0:LANES] = jnp.where(lo, vv, 0.0).astype(BF16)
            vbd_ref[n, 0:p, LANES:2 * LANES] = vf_ref[0]
            vbd_ref[n, p:2 * p, 0:LANES] = jnp.where(lo, 0.0, vv).astype(BF16)
            vbd_ref[n, p:2 * p, LANES:2 * LANES] = vf_ref[1]
            return carry

        lax.fori_loop(0, nb, build, 0)

    lhs = jnp.concatenate([q_ref[0], qf_ref[0]], axis=1)
    lane_a = _iota((p, 2 * LANES), 1)
    head0 = (lane_a < HEAD_DIM) | (lane_a == LANES)

    s = _dot_nt(lhs, kbd_ref[i])
    s = jnp.where((_iota((p, 2 * p), 1) & (p - 1)) <= _iota((p, 2 * p), 0), s, NEG)
    m0 = jnp.max(s[:, :p], axis=-1, keepdims=True)
    m1 = jnp.max(s[:, p:], axis=-1, keepdims=True)
    pexp = jnp.concatenate([jnp.exp(s[:, :p] - m0), jnp.exp(s[:, p:] - m1)], axis=1)
    acc = _dot(pexp.astype(BF16), vbd_ref[i])

    def body(n, carry):
        m0, m1, acc = carry
        s = _dot_nt(lhs, kbd_ref[n])
        n0 = jnp.maximum(m0, jnp.max(s[:, :p], axis=-1, keepdims=True))
        n1 = jnp.maximum(m1, jnp.max(s[:, p:], axis=-1, keepdims=True))
        pexp = jnp.concatenate([jnp.exp(s[:, :p] - n0), jnp.exp(s[:, p:] - n1)], axis=1)
        alpha = jnp.where(head0, jnp.exp(m0 - n0), jnp.exp(m1 - n1))
        return n0, n1, acc * alpha + _dot(pexp.astype(BF16), vbd_ref[n])

    _, _, acc = lax.fori_loop(0, i, body, (m0, m1, acc))
    l = jnp.where(_iota((p, LANES), 1) < HEAD_DIM, acc[:, LANES:LANES + 1], acc[:, LANES + 1:LANES + 2])
    o_ref[0] = (acc[:, :LANES] / l).astype(BF16)


def _moba_attention(q, qfeat, k, v, kf, vf):
    b, s, w = q.shape
    p = MOBA_BLOCK
    nb = s // p
    return pl.pallas_call(
        functools.partial(_moba_kernel, nb=nb),
        grid=(b, w // LANES, nb),
        in_specs=[pl.BlockSpec((1, p, LANES), lambda bi, hp, i: (bi, i, hp)),
                  pl.BlockSpec((1, p, LANES), lambda bi, hp, i: (bi, i, hp)),
                  pl.BlockSpec((1, s, LANES), lambda bi, hp, i: (bi, 0, hp)),
                  pl.BlockSpec((1, s, LANES), lambda bi, hp, i: (bi, 0, hp)),
                  _const_spec(kf.shape), _const_spec(vf.shape)],
        out_specs=pl.BlockSpec((1, p, LANES), lambda bi, hp, i: (bi, i, hp)),
        out_shape=jax.ShapeDtypeStruct((b, s, w), BF16),
        scratch_shapes=[pltpu.VMEM((nb, 2 * p, 2 * LANES), BF16),
                        pltpu.VMEM((nb, 2 * p, 2 * LANES), BF16)],
        compiler_params=_params(("parallel", "parallel", "arbitrary")),
    )(q, qfeat, k, v, kf, vf)


def kernel(x, even_w_in, even_kv_norm, even_w_uk, even_w_uv, even_w_out, odd_w_in, odd_w_out,
           ln_mix_g, ln_mix_b, ffn_gate, ffn_up, ffn_down, ln_ffn_g, ln_ffn_b):
    b, s, d = x.shape
    assert d == D_MODEL and s % (2 * MOBA_BLOCK) == 0 and s >= 4 * IDX_TOPK_MAX
    h = x.reshape(b * s, d)
    r3 = lambda t: t.reshape(b, s, t.shape[-1])
    for i in range(DEPTH):
        j = i // 2
        if i % 2 == 0:
            qlat, ckv, qidx, kidx, widx, qb, kb, vb = _proj_even(h, even_w_in[j], even_kv_norm[j], even_w_uk[j])
            o_a = _dsa_attention(r3(qlat), r3(qidx), r3(widx), r3(kidx), r3(ckv), even_w_uv[j])
            o_b = _stick_attention(r3(qb), r3(kb), r3(vb))
            w_out = even_w_out[j].astype(BF16)
            na = A_HEADS * HEAD_DIM
            outs = [o_a.reshape(b * s, -1), o_b.reshape(b * s, -1)]
            ws = [w_out[:na], w_out[na:]]
        else:
            nb = s // MOBA_BLOCK
            perm, srow, kf, vf = _moba_constants(nb, s)
            q, k, v, qfeat = _proj_odd(h, odd_w_in[j], perm, srow, nb)
            o_c = _moba_attention(r3(q), r3(qfeat), r3(k), r3(v), kf, vf)
            outs = [o_c.reshape(b * s, -1)]
            ws = [odd_w_out[j].astype(BF16)]
        h = _outproj_ln(outs, ws, h, ln_mix_g[i], ln_mix_b[i])
        h = _ffn_ln(h, ffn_gate[i], ffn_up[i], ffn_down[i], ln_ffn_g[i], ln_ffn_b[i])
    return h.reshape(b, s, d)
```

```python
import functools

import numpy as np
import jax
import jax.numpy as jnp
from jax import lax
from jax.experimental import pallas as pl
from jax.experimental.pallas import tpu as pltpu

D_MODEL = 1024
DEPTH = 2
HEAD_DIM = 64
A_HEADS = 8
A_KV_RANK = 128
IDX_HEADS = 4
IDX_DIM = 64
IDX_TOPK_MAX = 256
B_HEADS = 8
C_HEADS = 16
MOBA_BLOCK = 256
MOBA_TOPK = 3
D_FF = -(-8 * D_MODEL // (3 * 256)) * 256
DEEPNORM_ALPHA = (2 * DEPTH) ** 0.25
NEG = -1e30
LOG2E = 1.4426950408889634
SIGN_BIT = -2 ** 31

LANES = 128
VMEM_LIMIT = 52 * 1024 * 1024

BF16 = jnp.bfloat16
F32 = jnp.float32
I32 = jnp.int32


def _dot(a, b):
    return jnp.dot(a, b, preferred_element_type=F32)


def _dot_nt(a, b, precision=None):
    return lax.dot_general(a, b, (((1,), (1,)), ((), ())), precision=precision,
                           preferred_element_type=F32)


def _iota(shape, dim):
    return lax.broadcasted_iota(I32, shape, dim)


def _params(sem, vmem=VMEM_LIMIT):
    return pltpu.CompilerParams(dimension_semantics=sem, vmem_limit_bytes=vmem)


def _const_spec(shape):
    nd = len(shape)
    return pl.BlockSpec(shape, lambda *_: (0,) * nd)


def _layer_norm(y, g, b):
    mu = jnp.mean(y, axis=-1, keepdims=True)
    d = y - mu
    var = jnp.mean(d * d, axis=-1, keepdims=True)
    return d * lax.rsqrt(var + 1e-5) * g + b


def _proj_even_kernel(x_ref, w_ref, wuk_ref, g_ref,
                      qlat_ref, ckv_ref, qidx_ref, kidx_ref, widx_ref, qb_ref, kb_ref, vb_ref):
    xb = x_ref[...].astype(BF16)
    qa = _dot(xb, w_ref[:, 0:512]).astype(BF16)
    for p in range(A_HEADS // 2):
        ql = _dot(qa[:, 128 * p:128 * (p + 1)], wuk_ref[p])
        qlat_ref[:, 256 * p:256 * (p + 1)] = (ql * HEAD_DIM ** -0.5).astype(BF16)
    c = _dot(xb, w_ref[:, 512:640])
    ms = jnp.mean(c * c, axis=-1, keepdims=True)
    ckv_ref[...] = (c * lax.rsqrt(ms + 1e-6) * g_ref[...]).astype(BF16)
    qidx_ref[...] = (_dot(xb, w_ref[:, 640:896]) * IDX_DIM ** -0.5).astype(BF16)
    kidx_ref[...] = _dot(xb, w_ref[:, 896:1024]).astype(BF16)
    widx_ref[...] = _dot(xb, w_ref[:, 1024:1152]) * IDX_HEADS ** -0.5
    qb_ref[...] = (_dot(xb, w_ref[:, 1152:1664]) * HEAD_DIM ** -0.5).astype(BF16)
    kb_ref[...] = _dot(xb, w_ref[:, 1664:2176]).astype(BF16)
    vb_ref[...] = _dot(xb, w_ref[:, 2176:2688]).astype(BF16)


def _proj_even(x2, w_in, kv_norm, w_uk, tm=512):
    n = x2.shape[0]
    cuts = np.cumsum([512, 128, 256, 64, 4, 512, 512, 512])[:-1].tolist()
    q_a, c_kv, q_idx, k_idx, w_idx, q_b, k_b, v_b = jnp.split(w_in, cuts, axis=1)
    w_idx = jnp.pad(w_idx, ((0, 0), (0, LANES - IDX_HEADS)))
    w = jnp.concatenate([q_a, c_kv, q_idx, k_idx, k_idx, w_idx, q_b, k_b, v_b], axis=1).astype(BF16)
    z = jnp.zeros((HEAD_DIM, A_KV_RANK), F32)
    wuk = jnp.stack([jnp.block([[w_uk[2 * p], z], [z, w_uk[2 * p + 1]]]) for p in range(A_HEADS // 2)]).astype(BF16)
    g = kv_norm.reshape(1, A_KV_RANK)
    widths = [(1024, BF16), (128, BF16), (256, BF16), (128, BF16), (128, F32), (512, BF16), (512, BF16), (512, BF16)]
    return pl.pallas_call(
        _proj_even_kernel,
        grid=(n // tm,),
        in_specs=[pl.BlockSpec((tm, D_MODEL), lambda i: (i, 0)),
                  _const_spec(w.shape), _const_spec(wuk.shape), _const_spec(g.shape)],
        out_specs=[pl.BlockSpec((tm, wd), lambda i: (i, 0)) for wd, _ in widths],
        out_shape=[jax.ShapeDtypeStruct((n, wd), dt) for wd, dt in widths],
        compiler_params=_params(("parallel",)),
    )(x2, w, wuk, g)


def _dsa_kernel(qlat_ref, qidx_ref, widx_ref, kidx_ref, ckv_ref, kfeat_ref, wuv_ref, o_ref,
                key_ref, msk_ref, ckv2_ref, logit_ref, mrun_ref, acc_ref, olat_ref, *, tq, topk):
    i = pl.program_id(1)
    t0 = i * tq
    tk = tq
    hq = tq // 2
    nt = key_ref.shape[0]
    n_pair = (i + 2) // 2
    int_min = -2 ** 31

    @pl.when(i == 0)
    def _():
        ckv2_ref[:, 0:LANES] = ckv_ref[0]
        ckv2_ref[:, LANES:2 * LANES] = kfeat_ref[...]

    @pl.when(i + 1 < nt)
    def _():
        key_ref[i + 1] = jnp.full((tq, tk), int_min, I32)
        msk_ref[i + 1] = jnp.full((tq, tk), NEG, F32)

    rel = _iota((tq, tk), 1) - _iota((tq, tk), 0)
    lane_q = _iota((tq, LANES), 1)

    qm, wb = [], []
    for h in range(IDX_HEADS):
        p, half = divmod(h, 2)
        qi = qidx_ref[0, :, 128 * p:128 * (p + 1)].astype(F32)
        qm.append(jnp.where((lane_q < IDX_DIM) == (half == 0), qi, 0.0).astype(BF16))
        wb.append(jnp.broadcast_to(widx_ref[0, :, h:h + 1], (tq, tk)))

    def score_body(kt, carry):
        k0 = pl.multiple_of(kt * tk, tk)
        k2 = kidx_ref[0, pl.ds(k0, tk), :]
        acc = wb[0] * jnp.maximum(_dot_nt(qm[0], k2), 0.0)
        for h in range(1, IDX_HEADS):
            acc = acc + wb[h] * jnp.maximum(_dot_nt(qm[h], k2), 0.0)
        acc = jnp.where(rel <= t0 - k0, acc, NEG)
        bits = lax.bitcast_convert_type(acc, I32)
        key_ref[kt] = bits ^ ((bits >> 31) & 0x7FFFFFFF)
        return carry

    lax.fori_loop(0, i + 1, score_body, 0)

    def bit_body(bi, carry):
        cur, cfail = carry
        cand = cur + lax.shift_left(jnp.int32(1), 31 - bi)
        accs = []
        for half in range(2):
            ch = cand[half * hq:(half + 1) * hq]

            def cnt_body(kp, acc, ch=ch, half=half):
                for u in range(2):
                    kk = key_ref[2 * kp + u, half * hq:(half + 1) * hq, :]
                    acc = (acc + jnp.where(kk[:, :LANES] >= ch, 1.0, 0.0)
                           + jnp.where(kk[:, LANES:] >= ch, 1.0, 0.0))
                return acc

            accs.append(lax.fori_loop(0, n_pair, cnt_body, jnp.zeros((hq, LANES), F32)))
        cnt = jnp.sum(jnp.concatenate(accs, axis=0), axis=-1, keepdims=True)
        ok = cnt >= topk
        return jnp.where(ok, cand, cur), jnp.where(ok, cfail, cnt)

    thr, cnt_gt = lax.fori_loop(
        0, 32, bit_body,
        (jnp.full((tq, LANES), int_min, I32), jnp.zeros((tq, LANES), F32)))
    need = topk - cnt_gt
    thr2 = jnp.concatenate([thr, thr], axis=1)
    need2 = jnp.concatenate([need, need], axis=1)

    tri = jnp.where(_iota((tk, 2 * tk), 0) <= _iota((tk, 2 * tk), 1), 1.0, 0.0).astype(BF16)

    def mask_body(kt, seen):
        kk = key_ref[kt]
        eq = kk == thr2
        pc = _dot(jnp.where(eq, 1.0, 0.0).astype(BF16), tri)
        take = eq & (pc[:, :tk] + jnp.concatenate([seen, seen], axis=1) <= need2)
        sel = ((kk > thr2) | take) & (rel <= t0 - kt * tk)
        msk_ref[kt] = jnp.where(sel, 0.0, NEG)
        return seen + pc[:, tk:tk + LANES]

    lax.fori_loop(0, i + 1, mask_body, jnp.zeros((tq, LANES), F32))

    lane_h = _iota((hq, LANES), 1)
    for half in range(2):
        r0, r1 = half * hq, (half + 1) * hq
        t0h = (t0 + r0).astype(F32)
        parts = []
        for h in range(A_HEADS):
            slope = 2.0 ** (-8.0 * (h + 1) / A_HEADS)
            qf = jnp.where(lane_h < 2, slope, jnp.where(lane_h == 2, -slope * t0h, 0.0)).astype(BF16)
            parts.append(jnp.concatenate([qlat_ref[0, r0:r1, 128 * h:128 * (h + 1)], qf], axis=1))
        lhs = jnp.concatenate(parts, axis=0)
        mrun_ref[...] = jnp.full(mrun_ref.shape, NEG, F32)

        def pass1(kp, carry, lhs=lhs, r0=r0, r1=r1):
            k0 = pl.multiple_of(kp * 2 * tk, 2 * tk)
            s = _dot_nt(lhs, ckv2_ref[pl.ds(k0, 2 * tk), :])
            mk = [msk_ref[2 * kp + u, r0:r1, :] for u in range(2)]
            for h in range(A_HEADS):
                top = mrun_ref[h * hq:(h + 1) * hq, :]
                for u in range(2):
                    sh = s[h * hq:(h + 1) * hq, u * tk:(u + 1) * tk] + mk[u]
                    logit_ref[2 * kp + u, h * hq:(h + 1) * hq, :] = sh
                    top = jnp.maximum(top, jnp.maximum(sh[:, :LANES], sh[:, LANES:]))
                mrun_ref[h * hq:(h + 1) * hq, :] = top
            return carry

        lax.fori_loop(0, n_pair, pass1, 0)
        m = jnp.max(mrun_ref[...], axis=-1, keepdims=True)
        acc_ref[...] = jnp.zeros(acc_ref.shape, F32)

        def pass2(kp, carry, m=m):
            p0 = jnp.exp(logit_ref[2 * kp] - m).astype(BF16)
            p1 = jnp.exp(logit_ref[2 * kp + 1] - m).astype(BF16)
            k0 = pl.multiple_of(kp * 2 * tk, 2 * tk)
            acc_ref[...] += _dot(jnp.concatenate([p0, p1], axis=1), ckv2_ref[pl.ds(k0, 2 * tk), :])
            return carry

        lax.fori_loop(0, n_pair, pass2, 0)
        acc = acc_ref[...]
        o = acc[:, :LANES] / acc[:, LANES + 2:LANES + 3]
        for h in range(A_HEADS):
            olat_ref[r0:r1, 128 * h:128 * (h + 1)] = o[h * hq:(h + 1) * hq]

    for p in range(A_HEADS // 2):
        o = _dot(olat_ref[:, 256 * p:256 * (p + 1)].astype(BF16), wuv_ref[p])
        o_ref[0, :, 128 * p:128 * (p + 1)] = o.astype(BF16)


def _dsa_attention(qlat, qidx, widx, kidx, ckv, w_uv, tq=256):
    b, s, _ = qlat.shape
    topk = min(IDX_TOPK_MAX, s // 4)
    assert topk <= tq and s % (2 * tq) == 0
    z = jnp.zeros((A_KV_RANK, HEAD_DIM), F32)
    wuv = jnp.stack([jnp.block([[w_uv[2 * p], z], [z, w_uv[2 * p + 1]]]) for p in range(A_HEADS // 2)]).astype(BF16)
    pos = np.arange(s)
    kfeat = np.zeros((s, LANES), np.float32)
    kfeat[:, 0], kfeat[:, 1], kfeat[:, 2] = 256 * (pos // 256), pos % 256, 1.0
    kfeat = jnp.asarray(kfeat, BF16)
    nt = s // tq
    rows = A_HEADS * tq // 2
    return pl.pallas_call(
        functools.partial(_dsa_kernel, tq=tq, topk=topk),
        grid=(b, s // tq),
        in_specs=[pl.BlockSpec((1, tq, A_HEADS * A_KV_RANK), lambda bi, i: (bi, i, 0)),
                  pl.BlockSpec((1, tq, IDX_HEADS * IDX_DIM), lambda bi, i: (bi, i, 0)),
                  pl.BlockSpec((1, tq, LANES), lambda bi, i: (bi, i, 0)),
                  pl.BlockSpec((1, s, LANES), lambda bi, i: (bi, 0, 0)),
                  pl.BlockSpec((1, s, A_KV_RANK), lambda bi, i: (bi, 0, 0)),
                  _const_spec(kfeat.shape), _const_spec(wuv.shape)],
        out_specs=pl.BlockSpec((1, tq, A_HEADS * HEAD_DIM), lambda bi, i: (bi, i, 0)),
        out_shape=jax.ShapeDtypeStruct((b, s, A_HEADS * HEAD_DIM), BF16),
        scratch_shapes=[pltpu.VMEM((nt, tq, tq), I32),
                        pltpu.VMEM((nt, tq, tq), F32),
                        pltpu.VMEM((s, 2 * LANES), BF16),
                        pltpu.VMEM((nt, rows, tq), F32),
                        pltpu.VMEM((rows, LANES), F32),
                        pltpu.VMEM((rows, 2 * LANES), F32),
                        pltpu.VMEM((tq, A_HEADS * A_KV_RANK), F32)],
        compiler_params=_params(("parallel", "arbitrary")),
    )(qlat, qidx, widx, kidx, ckv, kfeat, wuv)


def _stick_kernel(q_ref, k_ref, v_ref, tri_ref, o_ref, kbd_ref, vbd_ref, c_ref, acc_ref, z_ref, a_ref, *, tq):
    i = pl.program_id(2)
    t0 = i * tq
    w2 = 2 * LANES
    nt = kbd_ref.shape[0] // w2
    sub = tq // LANES
    assert sub % 2 == 0

    @pl.when(i == 0)
    def _():
        lo = _iota((LANES, LANES), 1) < HEAD_DIM

        def build(kt, carry):
            r0 = pl.multiple_of(kt * LANES, LANES)
            d0 = pl.multiple_of(kt * w2, w2)
            kk = k_ref[0, pl.ds(r0, LANES), :].astype(F32)
            vv = v_ref[0, pl.ds(r0, LANES), :].astype(F32)
            kbd_ref[pl.ds(d0, LANES), :] = jnp.where(lo, kk, 0.0).astype(BF16)
            kbd_ref[pl.ds(d0 + LANES, LANES), :] = jnp.where(lo, 0.0, kk).astype(BF16)
            vbd_ref[pl.ds(d0, LANES), :] = jnp.where(lo, vv, 0.0).astype(BF16)
            vbd_ref[pl.ds(d0 + LANES, LANES), :] = jnp.where(lo, 0.0, vv).astype(BF16)
            return carry

        lax.fori_loop(0, nt, build, 0)

    q = q_ref[0]
    tri = tri_ref[...]
    c_ref[...] = jnp.zeros(c_ref.shape, F32)
    acc_ref[...] = jnp.zeros(acc_ref.shape, F32)

    def rows_of(kt):
        start = kt * w2
        return pl.ds(start if isinstance(start, int) else pl.multiple_of(start, 2 * w2), 2 * w2)

    def scores(kt):
        return _dot_nt(q, kbd_ref[rows_of(kt), :])

    def weights(z, kt, masked):
        z = z * LOG2E
        neg_abs = lax.bitcast_convert_type(lax.bitcast_convert_type(z, I32) | SIGN_BIT, F32)
        t = jnp.maximum(z, 0.0) + jnp.log2(1.0 + jnp.exp2(neg_abs))
        log_beta = z - t
        if masked:
            col = _iota((tq, 2 * w2), 1)
            key = kt * LANES + ((col >> 8) << 7) + (col & (LANES - 1))
            strict = key < t0 + _iota((tq, 2 * w2), 0)
            t = jnp.where(strict, t, 0.0)
        hi = t.astype(BF16)
        lo = (t - hi.astype(F32)).astype(BF16)
        lhs = jnp.concatenate(
            [jnp.concatenate([hi[:, g * LANES:(g + 1) * LANES], lo[:, g * LANES:(g + 1) * LANES]], axis=1)
             for g in range(4)], axis=0)
        r = _dot(lhs, tri)
        part = lambda g, c0: r[g * tq:(g + 1) * tq, c0:c0 + LANES]
        c_hi = c_ref[...]
        c_lo = c_hi + jnp.concatenate([part(2, LANES), part(3, LANES)], axis=1)
        rem = jnp.concatenate([jnp.concatenate([part(0, 0), part(1, 0)], axis=1) + c_lo,
                               jnp.concatenate([part(2, 0), part(3, 0)], axis=1) + c_hi], axis=1)
        a = jnp.exp2(log_beta + rem)
        if masked:
            a = jnp.where(strict, a, 0.0)
        c_ref[...] = c_lo + jnp.concatenate([part(0, LANES), part(1, LANES)], axis=1)
        return a.astype(BF16)

    def accumulate(a, kt):
        acc_ref[...] += _dot(a, vbd_ref[rows_of(kt), :])

    n_diag = sub // 2
    for d in range(n_diag):
        kt = (i + 1) * sub - 2 - 2 * d
        a = weights(scores(kt), kt, True)
        if d < n_diag - 1:
            accumulate(a, kt)
        else:
            a_ref[...] = a

    z_ref[...] = scores(jnp.maximum(i * sub - 2, 0))

    def body(j, carry):
        kt = i * sub - 2 - 2 * j
        z = z_ref[...]
        z_ref[...] = scores(jnp.maximum(kt - 2, 0))
        accumulate(a_ref[...], kt + 2)
        a_ref[...] = weights(z, kt, False)
        return carry

    lax.fori_loop(0, i * n_diag, body, 0)
    accumulate(a_ref[...], 0)
    o_ref[0] = acc_ref[...].astype(BF16)


def _stick_attention(q, k, v, tq=512):
    b, s, w = q.shape
    jj = np.arange(2 * LANES)[:, None] % LANES
    ss = np.arange(2 * LANES)[None, :]
    tri = jnp.asarray(-((ss >= LANES) | (jj > ss)).astype(np.float32), BF16)
    nt = s // LANES
    return pl.pallas_call(
        functools.partial(_stick_kernel, tq=tq),
        grid=(b, w // LANES, s // tq),
        in_specs=[pl.BlockSpec((1, tq, LANES), lambda bi, p, i: (bi, i, p)),
                  pl.BlockSpec((1, s, LANES), lambda bi, p, i: (bi, 0, p)),
                  pl.BlockSpec((1, s, LANES), lambda bi, p, i: (bi, 0, p)),
                  _const_spec(tri.shape)],
        out_specs=pl.BlockSpec((1, tq, LANES), lambda bi, p, i: (bi, i, p)),
        out_shape=jax.ShapeDtypeStruct((b, s, w), BF16),
        scratch_shapes=[pltpu.VMEM((nt * 2 * LANES, LANES), BF16),
                        pltpu.VMEM((nt * 2 * LANES, LANES), BF16),
                        pltpu.VMEM((tq, 2 * LANES), F32),
                        pltpu.VMEM((tq, LANES), F32),
                        pltpu.VMEM((tq, 4 * LANES), F32),
                        pltpu.VMEM((tq, 4 * LANES), BF16)],
        compiler_params=_params(("parallel", "parallel", "arbitrary")),
    )(q, k, v, tri)


def _outproj_ln_kernel(*refs, n_in):
    o_refs, w_refs = refs[:n_in], refs[n_in:2 * n_in]
    x_ref, g_ref, b_ref, out_ref = refs[2 * n_in:]
    m = _dot(o_refs[0][...], w_refs[0][...])
    for o_ref, w_ref in zip(o_refs[1:], w_refs[1:]):
        m = m + _dot(o_ref[...], w_ref[...])
    out_ref[...] = _layer_norm(DEEPNORM_ALPHA * x_ref[...] + m, g_ref[...], b_ref[...])


def _outproj_ln(os_, ws, x2, g, b, tm=512):
    n = x2.shape[0]
    n_in = len(os_)
    return pl.pallas_call(
        functools.partial(_outproj_ln_kernel, n_in=n_in),
        grid=(n // tm,),
        in_specs=([pl.BlockSpec((tm, o.shape[1]), lambda i: (i, 0)) for o in os_]
                  + [_const_spec(w.shape) for w in ws]
                  + [pl.BlockSpec((tm, D_MODEL), lambda i: (i, 0)),
                     _const_spec((1, D_MODEL)), _const_spec((1, D_MODEL))]),
        out_specs=pl.BlockSpec((tm, D_MODEL), lambda i: (i, 0)),
        out_shape=jax.ShapeDtypeStruct((n, D_MODEL), F32),
        compiler_params=_params(("parallel",)),
    )(*os_, *ws, x2, g.reshape(1, D_MODEL), b.reshape(1, D_MODEL))


def _ffn_ln_kernel(x_ref, wg_ref, wu_ref, wd_ref, g_ref, b_ref, out_ref, acc_ref, *, fc):
    x = x_ref[...]
    xb = x.astype(BF16)
    for c in range(D_FF // fc):
        gate = _dot(xb, wg_ref[:, c * fc:(c + 1) * fc])
        up = _dot(xb, wu_ref[:, c * fc:(c + 1) * fc])
        act = (gate * jax.nn.sigmoid(gate) * up).astype(BF16)
        d = _dot(act, wd_ref[c * fc:(c + 1) * fc, :])
        if c == 0:
            acc_ref[...] = d
        else:
            acc_ref[...] += d
    out_ref[...] = _layer_norm(DEEPNORM_ALPHA * x + acc_ref[...], g_ref[...], b_ref[...])


def _ffn_ln(x2, wg, wu, wd, g, b, tm=512, fc=256):
    n = x2.shape[0]
    single = dict(pipeline_mode=pl.Buffered(1))
    return pl.pallas_call(
        functools.partial(_ffn_ln_kernel, fc=fc),
        grid=(n // tm,),
        in_specs=[pl.BlockSpec((tm, D_MODEL), lambda i: (i, 0)),
                  pl.BlockSpec((D_MODEL, D_FF), lambda i: (0, 0), **single),
                  pl.BlockSpec((D_MODEL, D_FF), lambda i: (0, 0), **single),
                  pl.BlockSpec((D_FF, D_MODEL), lambda i: (0, 0), **single),
                  _const_spec((1, D_MODEL)), _const_spec((1, D_MODEL))],
        out_specs=pl.BlockSpec((tm, D_MODEL), lambda i: (i, 0)),
        out_shape=jax.ShapeDtypeStruct((n, D_MODEL), F32),
        scratch_shapes=[pltpu.VMEM((tm, D_MODEL), F32)],
        compiler_params=_params(("parallel",)),
    )(x2, wg.astype(BF16), wu.astype(BF16), wd.astype(BF16), g.reshape(1, D_MODEL), b.reshape(1, D_MODEL))


MOBA_SEL_LANE = 0
MOBA_POS_LANE = 32
MOBA_GROUP = 2


def _proj_odd_kernel(x_ref, w_ref, perm_ref, slope_ref, q_ref, k_ref, v_ref, qf_ref, kme_ref, *, nb):
    step = pl.program_id(0)
    n = step % nb
    nh = C_HEADS
    wq = nh * HEAD_DIM

    @pl.when(step == 0)
    def _():
        kme_ref[...] = jnp.zeros_like(kme_ref)

    xb = x_ref[...].astype(BF16)
    qf = _dot(xb, w_ref[:, 0:wq])
    kf = _dot(xb, w_ref[:, wq:2 * wq])
    q_ref[...] = (qf * HEAD_DIM ** -0.5).astype(BF16)
    k_ref[...] = kf.astype(BF16)
    v_ref[...] = _dot(xb, w_ref[:, 2 * wq:3 * wq]).astype(BF16)

    gate = _dot_nt(kme_ref[...], qf, precision=lax.Precision.HIGHEST)
    blk = _iota(gate.shape, 0) // nh
    gate = jnp.where(blk < n, gate, NEG)
    gs = [gate[nh * m:nh * (m + 1), :] for m in range(nb)]
    biases = []
    for a in range(nb):
        rank = jnp.zeros_like(gs[a])
        for m in range(nb):
            if m != a:
                beats = (gs[m] >= gs[a]) if m < a else (gs[m] > gs[a])
                rank = rank + jnp.where(beats, 1.0, 0.0)
        biases.append(jnp.where(rank < MOBA_TOPK, 0.0, NEG))
    bias = jnp.where(blk < n, jnp.concatenate(biases, axis=0), jnp.where(blk == n, 0.0, NEG))
    qf_ref[...] = (_dot(bias.T.astype(BF16), perm_ref[...]) + slope_ref[...]).astype(BF16)

    kmean = jnp.mean(kf, axis=0, keepdims=True)
    head = _iota(kmean.shape, 1) // HEAD_DIM
    for h in range(nh):
        kme_ref[pl.ds(n * nh + h, 1), :] = jnp.where(head == h, kmean, 0.0)


def _bf16_pieces(x):
    x = np.float32(x)
    out = []
    for _ in range(3):
        p = np.float32(np.asarray(x, dtype=BF16))
        out.append(p)
        x = np.float32(x - p)
    assert x == 0.0
    return out


def _moba_constants(nb, s):
    nh, p = C_HEADS, MOBA_BLOCK
    assert 2 * nb <= MOBA_POS_LANE and MOBA_POS_LANE + 12 <= LANES
    slopes = np.asarray(2.0 ** (-8.0 * np.arange(1, nh + 1) / nh), np.float32)
    perm = np.zeros((nb * nh, nh // 2 * LANES), np.float32)
    srow = np.zeros((1, nh // 2 * LANES), np.float32)
    for h in range(nh):
        base = (h // 2) * LANES
        for m in range(nb):
            perm[m * nh + h, base + MOBA_SEL_LANE + (h % 2) * nb + m] = 1.0
        pieces = _bf16_pieces(slopes[h])
        for c in range(6):
            srow[0, base + MOBA_POS_LANE + 6 * (h % 2) + c] = pieces[c % 3]
    pos = np.arange(s)
    kf = np.zeros((2, s, LANES), np.float32)
    for half in range(2):
        kf[half, pos, MOBA_SEL_LANE + half * nb + pos // p] = 1.0
        for c in range(3):
            kf[half, :, MOBA_POS_LANE + 6 * half + c] = pos % p
            kf[half, :, MOBA_POS_LANE + 6 * half + 3 + c] = p * (pos // p)
    vf = np.zeros((2, p, LANES), np.float32)
    vf[0, :, 0] = 1.0
    vf[1, :, 1] = 1.0
    return (jnp.asarray(perm, BF16), jnp.asarray(srow, F32), jnp.asarray(kf, BF16), jnp.asarray(vf, BF16))


def _proj_odd(x2, w_in, perm, srow, nb):
    n = x2.shape[0]
    p = MOBA_BLOCK
    wq = C_HEADS * HEAD_DIM
    w = w_in.astype(BF16)
    fw = perm.shape[1]
    return pl.pallas_call(
        functools.partial(_proj_odd_kernel, nb=nb),
        grid=(n // p,),
        in_specs=[pl.BlockSpec((p, D_MODEL), lambda i: (i, 0)), _const_spec(w.shape),
                  _const_spec(perm.shape), _const_spec(srow.shape)],
        out_specs=[pl.BlockSpec((p, wq), lambda i: (i, 0))] * 3 + [pl.BlockSpec((p, fw), lambda i: (i, 0))],
        out_shape=[jax.ShapeDtypeStruct((n, wq), BF16)] * 3 + [jax.ShapeDtypeStruct((n, fw), BF16)],
        scratch_shapes=[pltpu.VMEM((nb * C_HEADS, wq), F32)],
        compiler_params=_params(("arbitrary",)),
    )(x2, w, perm, srow)


def _moba_kernel(q_ref, qf_ref, k_ref, v_ref, kf_ref, vf_ref, o_ref, kbd_ref, vbd_ref, s_ref, *, nb):
    i = pl.program_id(2)
    p = MOBA_BLOCK

    @pl.when(i == 0)
    def _():
        lo = _iota((p, LANES), 1) < HEAD_DIM

        def build(n, carry):
            r0 = pl.multiple_of(n * p, p)
            d0 = pl.multiple_of(n * 2 * p, 2 * p)
            kk = k_ref[0, pl.ds(r0, p), :].astype(F32)
            vv = v_ref[0, pl.ds(r0, p), :].astype(F32)
            kbd_ref[pl.ds(d0, p), 0:LANES] = jnp.where(lo, kk, 0.0).astype(BF16)
            kbd_ref[pl.ds(d0, p), LANES:2 * LANES] = kf_ref[0, pl.ds(r0, p), :]
            kbd_ref[pl.ds(d0 + p, p), 0:LANES] = jnp.where(lo, 0.0, kk).astype(BF16)
            kbd_ref[pl.ds(d0 + p, p), LANES:2 * LANES] = kf_ref[1, pl.ds(r0, p), :]
            vbd_ref[pl.ds(d0, p), 0:LANES] = jnp.where(lo, vv, 0.0).astype(BF16)
            vbd_ref[pl.ds(d0, p), LANES:2 * LANES] = vf_ref[0]
            vbd_ref[pl.ds(d0 + p, p), 0:LANES] = jnp.where(lo, 0.0, vv).astype(BF16)
            vbd_ref[pl.ds(d0 + p, p), LANES:2 * LANES] = vf_ref[1]
            return carry

        lax.fori_loop(0, nb, build, 0)

    tq = MOBA_GROUP * p
    gw = MOBA_GROUP * 2 * p
    lhs = jnp.concatenate([q_ref[0], qf_ref[0]], axis=1)
    lane_a = _iota((tq, 2 * LANES), 1)
    head0 = (lane_a < HEAD_DIM) | (lane_a == LANES)

    def scores(g):
        return _dot_nt(lhs, kbd_ref[pl.ds(pl.multiple_of(g * gw, gw), gw), :])

    def attend(g, s, carry, masked):
        m0, m1, acc = carry
        if masked:
            col = _iota((tq, gw), 1)
            key = ((col >> 9) << 8) + (col & (p - 1))
            s = jnp.where(key <= _iota((tq, gw), 0), s, NEG)
        s0 = [s[:, u * 2 * p:u * 2 * p + p] for u in range(MOBA_GROUP)]
        s1 = [s[:, u * 2 * p + p:(u + 1) * 2 * p] for u in range(MOBA_GROUP)]
        n0 = jnp.maximum(m0, jnp.max(functools.reduce(jnp.maximum, s0), axis=-1, keepdims=True))
        n1 = jnp.maximum(m1, jnp.max(functools.reduce(jnp.maximum, s1), axis=-1, keepdims=True))
        pexp = jnp.concatenate(
            [e for u in range(MOBA_GROUP) for e in (jnp.exp(s0[u] - n0), jnp.exp(s1[u] - n1))], axis=1)
        alpha = jnp.where(head0, jnp.exp(m0 - n0), jnp.exp(m1 - n1))
        pv = _dot(pexp.astype(BF16), vbd_ref[pl.ds(pl.multiple_of(g * gw, gw), gw), :])
        return n0, n1, acc * alpha + pv

    s_ref[...] = scores(0)

    def body(g, carry):
        s = s_ref[...]
        s_ref[...] = scores(g + 1)
        return attend(g, s, carry, False)

    init = (jnp.full((tq, 1), NEG, F32), jnp.full((tq, 1), NEG, F32), jnp.zeros((tq, 2 * LANES), F32))
    carry = lax.fori_loop(0, i, body, init)
    _, _, acc = attend(i, s_ref[...], carry, True)
    l = jnp.where(_iota((tq, LANES), 1) < HEAD_DIM, acc[:, LANES:LANES + 1], acc[:, LANES + 1:LANES + 2])
    o_ref[0] = (acc[:, :LANES] / l).astype(BF16)


def _moba_attention(q, qfeat, k, v, kf, vf):
    b, s, w = q.shape
    p = MOBA_BLOCK
    nb = s // p
    tq = MOBA_GROUP * p
    assert nb % MOBA_GROUP == 0
    return pl.pallas_call(
        functools.partial(_moba_kernel, nb=nb),
        grid=(b, w // LANES, s // tq),
        in_specs=[pl.BlockSpec((1, tq, LANES), lambda bi, hp, i: (bi, i, hp)),
                  pl.BlockSpec((1, tq, LANES), lambda bi, hp, i: (bi, i, hp)),
                  pl.BlockSpec((1, s, LANES), lambda bi, hp, i: (bi, 0, hp)),
                  pl.BlockSpec((1, s, LANES), lambda bi, hp, i: (bi, 0, hp)),
                  _const_spec(kf.shape), _const_spec(vf.shape)],
        out_specs=pl.BlockSpec((1, tq, LANES), lambda bi, hp, i: (bi, i, hp)),
        out_shape=jax.ShapeDtypeStruct((b, s, w), BF16),
        scratch_shapes=[pltpu.VMEM((nb * 2 * p, 2 * LANES), BF16),
                        pltpu.VMEM((nb * 2 * p, 2 * LANES), BF16),
                        pltpu.VMEM((tq, MOBA_GROUP * 2 * p), F32)],
        compiler_params=_params(("parallel", "parallel", "arbitrary")),
    )(q, qfeat, k, v, kf, vf)


def kernel(x, even_w_in, even_kv_norm, even_w_uk, even_w_uv, even_w_out, odd_w_in, odd_w_out,
           ln_mix_g, ln_mix_b, ffn_gate, ffn_up, ffn_down, ln_ffn_g, ln_ffn_b):
    b, s, d = x.shape
    assert d == D_MODEL and s % (2 * MOBA_BLOCK) == 0 and s >= 4 * IDX_TOPK_MAX
    h = x.reshape(b * s, d)
    r3 = lambda t: t.reshape(b, s, t.shape[-1])
    for i in range(DEPTH):
        j = i // 2
        if i % 2 == 0:
            qlat, ckv, qidx, kidx, widx, qb, kb, vb = _proj_even(h, even_w_in[j], even_kv_norm[j], even_w_uk[j])
            o_a = _dsa_attention(r3(qlat), r3(qidx), r3(widx), r3(kidx), r3(ckv), even_w_uv[j])
            o_b = _stick_attention(r3(qb), r3(kb), r3(vb))
            w_out = even_w_out[j].astype(BF16)
            na = A_HEADS * HEAD_DIM
            outs = [o_a.reshape(b * s, -1), o_b.reshape(b * s, -1)]
            ws = [w_out[:na], w_out[na:]]
        else:
            nb = s // MOBA_BLOCK
            perm, srow, kf, vf = _moba_constants(nb, s)
            q, k, v, qfeat = _proj_odd(h, odd_w_in[j], perm, srow, nb)
            o_c = _moba_attention(r3(q), r3(qfeat), r3(k), r3(v), kf, vf)
            outs = [o_c.reshape(b * s, -1)]
            ws = [odd_w_out[j].astype(BF16)]
        h = _outproj_ln(outs, ws, h, ln_mix_g[i], ln_mix_b[i])
        h = _ffn_ln(h, ffn_gate[i], ffn_up[i], ffn_down[i], ln_ffn_g[i], ln_ffn_b[i])
    return h.reshape(b, s, d)
```

```python
import functools

import numpy as np
import jax
import jax.numpy as jnp
from jax import lax
from jax.experimental import pallas as pl
from jax.experimental.pallas import tpu as pltpu

D_MODEL = 1024
DEPTH = 2
HEAD_DIM = 64
A_HEADS = 8
A_KV_RANK = 128
IDX_HEADS = 4
IDX_DIM = 64
IDX_TOPK_MAX = 256
B_HEADS = 8
C_HEADS = 16
MOBA_BLOCK = 256
MOBA_TOPK = 3
D_FF = -(-8 * D_MODEL // (3 * 256)) * 256
DEEPNORM_ALPHA = (2 * DEPTH) ** 0.25
NEG = -1e30
LOG2E = 1.4426950408889634
SIGN_BIT = -2 ** 31

LANES = 128
VMEM_LIMIT = 52 * 1024 * 1024

BF16 = jnp.bfloat16
F32 = jnp.float32
I32 = jnp.int32


def _dot(a, b):
    return jnp.dot(a, b, preferred_element_type=F32)


def _dot_nt(a, b, precision=None):
    return lax.dot_general(a, b, (((1,), (1,)), ((), ())), precision=precision,
                           preferred_element_type=F32)


def _iota(shape, dim):
    return lax.broadcasted_iota(I32, shape, dim)


def _params(sem, vmem=VMEM_LIMIT):
    return pltpu.CompilerParams(dimension_semantics=sem, vmem_limit_bytes=vmem)


def _const_spec(shape):
    nd = len(shape)
    return pl.BlockSpec(shape, lambda *_: (0,) * nd)


def _layer_norm(y, g, b):
    mu = jnp.mean(y, axis=-1, keepdims=True)
    d = y - mu
    var = jnp.mean(d * d, axis=-1, keepdims=True)
    return d * lax.rsqrt(var + 1e-5) * g + b


def _proj_even_kernel(x_ref, w_ref, wuk_ref, g_ref,
                      qlat_ref, ckv_ref, qidx_ref, kidx_ref, widx_ref, qb_ref, kb_ref, vb_ref):
    xb = x_ref[...].astype(BF16)
    qa = _dot(xb, w_ref[:, 0:512]).astype(BF16)
    for p in range(A_HEADS // 2):
        ql = _dot(qa[:, 128 * p:128 * (p + 1)], wuk_ref[p])
        qlat_ref[:, 256 * p:256 * (p + 1)] = (ql * HEAD_DIM ** -0.5).astype(BF16)
    c = _dot(xb, w_ref[:, 512:640])
    ms = jnp.mean(c * c, axis=-1, keepdims=True)
    ckv_ref[...] = (c * lax.rsqrt(ms + 1e-6) * g_ref[...]).astype(BF16)
    qidx_ref[...] = (_dot(xb, w_ref[:, 640:896]) * IDX_DIM ** -0.5).astype(BF16)
    kidx_ref[...] = _dot(xb, w_ref[:, 896:1024]).astype(BF16)
    widx_ref[...] = _dot(xb, w_ref[:, 1024:1152]) * IDX_HEADS ** -0.5
    qb_ref[...] = (_dot(xb, w_ref[:, 1152:1664]) * HEAD_DIM ** -0.5).astype(BF16)
    kb_ref[...] = _dot(xb, w_ref[:, 1664:2176]).astype(BF16)
    vb_ref[...] = _dot(xb, w_ref[:, 2176:2688]).astype(BF16)


def _proj_even(x2, w_in, kv_norm, w_uk, tm=512):
    n = x2.shape[0]
    cuts = np.cumsum([512, 128, 256, 64, 4, 512, 512, 512])[:-1].tolist()
    q_a, c_kv, q_idx, k_idx, w_idx, q_b, k_b, v_b = jnp.split(w_in, cuts, axis=1)
    w_idx = jnp.pad(w_idx, ((0, 0), (0, LANES - IDX_HEADS)))
    w = jnp.concatenate([q_a, c_kv, q_idx, k_idx, k_idx, w_idx, q_b, k_b, v_b], axis=1).astype(BF16)
    z = jnp.zeros((HEAD_DIM, A_KV_RANK), F32)
    wuk = jnp.stack([jnp.block([[w_uk[2 * p], z], [z, w_uk[2 * p + 1]]]) for p in range(A_HEADS // 2)]).astype(BF16)
    g = kv_norm.reshape(1, A_KV_RANK)
    widths = [(1024, BF16), (128, BF16), (256, BF16), (128, BF16), (128, F32), (512, BF16), (512, BF16), (512, BF16)]
    return pl.pallas_call(
        _proj_even_kernel,
        grid=(n // tm,),
        in_specs=[pl.BlockSpec((tm, D_MODEL), lambda i: (i, 0)),
                  _const_spec(w.shape), _const_spec(wuk.shape), _const_spec(g.shape)],
        out_specs=[pl.BlockSpec((tm, wd), lambda i: (i, 0)) for wd, _ in widths],
        out_shape=[jax.ShapeDtypeStruct((n, wd), dt) for wd, dt in widths],
        compiler_params=_params(("parallel",)),
    )(x2, w, wuk, g)


def _dsa_kernel(qlat_ref, qidx_ref, widx_ref, kidx_ref, ckv_ref, kfeat_ref, wuv_ref, o_ref,
                key_ref, msk_ref, ckv2_ref, logit_ref, mrun_ref, acc_ref, olat_ref, *, tq, topk):
    i = pl.program_id(1)
    t0 = i * tq
    tk = tq
    hq = tq // 2
    nt = key_ref.shape[0]
    n_pair = (i + 2) // 2
    int_min = -2 ** 31

    @pl.when(i == 0)
    def _():
        ckv2_ref[:, 0:LANES] = ckv_ref[0]
        ckv2_ref[:, LANES:2 * LANES] = kfeat_ref[...]

    rel = _iota((tk, tq), 0) - _iota((tk, tq), 1)
    lane_q = _iota((tq, LANES), 1)

    qm = []
    for h in range(IDX_HEADS):
        p, half = divmod(h, 2)
        qi = qidx_ref[0, :, 128 * p:128 * (p + 1)].astype(F32)
        qm.append(jnp.where((lane_q < IDX_DIM) == (half == 0), qi, 0.0).astype(BF16))
    w_t = widx_ref[0].T
    w_row = [w_t[h:h + 1, :] for h in range(IDX_HEADS)]

    def score_body(kp, carry):
        for u in range(2):
            kt = 2 * kp + u
            k0 = pl.multiple_of(kt * tk, tk)
            k2 = kidx_ref[0, pl.ds(k0, tk), :]
            acc = w_row[0] * jnp.maximum(_dot_nt(k2, qm[0]), 0.0)
            for h in range(1, IDX_HEADS):
                acc = acc + w_row[h] * jnp.maximum(_dot_nt(k2, qm[h]), 0.0)
            acc = jnp.where(rel <= t0 - k0, acc, NEG)
            bits = lax.bitcast_convert_type(acc, I32)
            key_ref[kt] = bits ^ ((bits >> 31) & 0x7FFFFFFF)
        return carry

    lax.fori_loop(0, n_pair, score_body, 0)

    fold = 64

    def bit_body(bi, carry):
        cur, cfail = carry
        cand = cur + lax.shift_left(jnp.int32(1), 31 - bi)

        def cnt_body(kp, acc):
            for u in range(2):
                hit = jnp.where(key_ref[2 * kp + u] >= cand, 1.0, 0.0)
                acc = acc + jnp.sum(hit.reshape(tk // fold, fold, tq), axis=0)
            return acc

        acc = lax.fori_loop(0, n_pair, cnt_body, jnp.zeros((fold, tq), F32))
        cnt = jnp.sum(acc, axis=0, keepdims=True)
        ok = cnt >= topk
        return jnp.where(ok, cand, cur), jnp.where(ok, cfail, cnt)

    thr, cnt_gt = lax.fori_loop(
        0, 32, bit_body, (jnp.full((1, tq), int_min, I32), jnp.zeros((1, tq), F32)))
    need = topk - cnt_gt

    ltri = jnp.where(_iota((tk, tk), 0) >= _iota((tk, tk), 1), 1.0, 0.0).astype(BF16)

    def mask_body(kp, seen):
        for u in range(2):
            kt = 2 * kp + u
            kk = key_ref[kt]
            eq = kk == thr
            pc = _dot(ltri, jnp.where(eq, 1.0, 0.0).astype(BF16))
            take = eq & (pc + seen <= need)
            sel = ((kk > thr) | take) & (rel <= t0 - kt * tk)
            msk_ref[kt] = jnp.where(sel, 0.0, NEG).T
            seen = seen + pc[tk - 1:tk, :]
        return seen

    lax.fori_loop(0, n_pair, mask_body, jnp.zeros((1, tq), F32))

    lane_h = _iota((hq, LANES), 1)
    for half in range(2):
        r0, r1 = half * hq, (half + 1) * hq
        t0h = (t0 + r0).astype(F32)
        parts = []
        for h in range(A_HEADS):
            slope = 2.0 ** (-8.0 * (h + 1) / A_HEADS)
            qf = jnp.where(lane_h < 2, slope, jnp.where(lane_h == 2, -slope * t0h, 0.0)).astype(BF16)
            parts.append(jnp.concatenate([qlat_ref[0, r0:r1, 128 * h:128 * (h + 1)], qf], axis=1))
        lhs = jnp.concatenate(parts, axis=0)
        mrun_ref[...] = jnp.full(mrun_ref.shape, NEG, F32)

        def pass1(kp, carry, lhs=lhs, r0=r0, r1=r1):
            k0 = pl.multiple_of(kp * 2 * tk, 2 * tk)
            s = _dot_nt(lhs, ckv2_ref[pl.ds(k0, 2 * tk), :])
            mk = [msk_ref[2 * kp + u, r0:r1, :] for u in range(2)]
            for h in range(A_HEADS):
                top = mrun_ref[h * hq:(h + 1) * hq, :]
                for u in range(2):
                    sh = s[h * hq:(h + 1) * hq, u * tk:(u + 1) * tk] + mk[u]
                    logit_ref[2 * kp + u, h * hq:(h + 1) * hq, :] = sh
                    top = jnp.maximum(top, jnp.maximum(sh[:, :LANES], sh[:, LANES:]))
                mrun_ref[h * hq:(h + 1) * hq, :] = top
            return carry

        lax.fori_loop(0, n_pair, pass1, 0)
        m = jnp.max(mrun_ref[...], axis=-1, keepdims=True)
        acc_ref[...] = jnp.zeros(acc_ref.shape, F32)

        def pass2(kp, carry, m=m):
            p0 = jnp.exp(logit_ref[2 * kp] - m).astype(BF16)
            p1 = jnp.exp(logit_ref[2 * kp + 1] - m).astype(BF16)
            k0 = pl.multiple_of(kp * 2 * tk, 2 * tk)
            acc_ref[...] += _dot(jnp.concatenate([p0, p1], axis=1), ckv2_ref[pl.ds(k0, 2 * tk), :])
            return carry

        lax.fori_loop(0, n_pair, pass2, 0)
        acc = acc_ref[...]
        o = acc[:, :LANES] / acc[:, LANES + 2:LANES + 3]
        for h in range(A_HEADS):
            olat_ref[r0:r1, 128 * h:128 * (h + 1)] = o[h * hq:(h + 1) * hq]

    for p in range(A_HEADS // 2):
        o = _dot(olat_ref[:, 256 * p:256 * (p + 1)].astype(BF16), wuv_ref[p])
        o_ref[0, :, 128 * p:128 * (p + 1)] = o.astype(BF16)


def _dsa_attention(qlat, qidx, widx, kidx, ckv, w_uv, tq=256):
    b, s, _ = qlat.shape
    topk = min(IDX_TOPK_MAX, s // 4)
    assert topk <= tq and s % (2 * tq) == 0
    z = jnp.zeros((A_KV_RANK, HEAD_DIM), F32)
    wuv = jnp.stack([jnp.block([[w_uv[2 * p], z], [z, w_uv[2 * p + 1]]]) for p in range(A_HEADS // 2)]).astype(BF16)
    pos = np.arange(s)
    kfeat = np.zeros((s, LANES), np.float32)
    kfeat[:, 0], kfeat[:, 1], kfeat[:, 2] = 256 * (pos // 256), pos % 256, 1.0
    kfeat = jnp.asarray(kfeat, BF16)
    nt = s // tq
    rows = A_HEADS * tq // 2
    return pl.pallas_call(
        functools.partial(_dsa_kernel, tq=tq, topk=topk),
        grid=(b, s // tq),
        in_specs=[pl.BlockSpec((1, tq, A_HEADS * A_KV_RANK), lambda bi, i: (bi, i, 0)),
                  pl.BlockSpec((1, tq, IDX_HEADS * IDX_DIM), lambda bi, i: (bi, i, 0)),
                  pl.BlockSpec((1, tq, LANES), lambda bi, i: (bi, i, 0)),
                  pl.BlockSpec((1, s, LANES), lambda bi, i: (bi, 0, 0)),
                  pl.BlockSpec((1, s, A_KV_RANK), lambda bi, i: (bi, 0, 0)),
                  _const_spec(kfeat.shape), _const_spec(wuv.shape)],
        out_specs=pl.BlockSpec((1, tq, A_HEADS * HEAD_DIM), lambda bi, i: (bi, i, 0)),
        out_shape=jax.ShapeDtypeStruct((b, s, A_HEADS * HEAD_DIM), BF16),
        scratch_shapes=[pltpu.VMEM((nt, tq, tq), I32),
                        pltpu.VMEM((nt, tq, tq), F32),
                        pltpu.VMEM((s, 2 * LANES), BF16),
                        pltpu.VMEM((nt, rows, tq), F32),
                        pltpu.VMEM((rows, LANES), F32),
                        pltpu.VMEM((rows, 2 * LANES), F32),
                        pltpu.VMEM((tq, A_HEADS * A_KV_RANK), F32)],
        compiler_params=_params(("parallel", "arbitrary")),
    )(qlat, qidx, widx, kidx, ckv, kfeat, wuv)


def _stick_kernel(q_ref, k_ref, v_ref, tri_ref, o_ref, kbd_ref, vbd_ref, c_ref, acc_ref, z_ref, a_ref, *, tq):
    i = pl.program_id(2)
    t0 = i * tq
    w2 = 2 * LANES
    nt = kbd_ref.shape[0] // w2
    sub = tq // LANES
    assert sub % 2 == 0

    @pl.when(i == 0)
    def _():
        lo = _iota((LANES, LANES), 1) < HEAD_DIM

        def build(kt, carry):
            r0 = pl.multiple_of(kt * LANES, LANES)
            d0 = pl.multiple_of(kt * w2, w2)
            kk = k_ref[0, pl.ds(r0, LANES), :].astype(F32)
            vv = v_ref[0, pl.ds(r0, LANES), :].astype(F32)
            kbd_ref[pl.ds(d0, LANES), :] = jnp.where(lo, kk, 0.0).astype(BF16)
            kbd_ref[pl.ds(d0 + LANES, LANES), :] = jnp.where(lo, 0.0, kk).astype(BF16)
            vbd_ref[pl.ds(d0, LANES), :] = jnp.where(lo, vv, 0.0).astype(BF16)
            vbd_ref[pl.ds(d0 + LANES, LANES), :] = jnp.where(lo, 0.0, vv).astype(BF16)
            return carry

        lax.fori_loop(0, nt, build, 0)

    q = q_ref[0]
    tri = tri_ref[...]
    c_ref[...] = jnp.zeros(c_ref.shape, F32)
    acc_ref[...] = jnp.zeros(acc_ref.shape, F32)

    def rows_of(kt):
        start = kt * w2
        return pl.ds(start if isinstance(start, int) else pl.multiple_of(start, 2 * w2), 2 * w2)

    def scores(kt):
        return _dot_nt(q, kbd_ref[rows_of(kt), :])

    def weights(z, kt, masked):
        z = z * LOG2E
        neg_abs = lax.bitcast_convert_type(lax.bitcast_convert_type(z, I32) | SIGN_BIT, F32)
        t = jnp.maximum(z, 0.0) + jnp.log2(1.0 + jnp.exp2(neg_abs))
        log_beta = z - t
        if masked:
            col = _iota((tq, 2 * w2), 1)
            key = kt * LANES + ((col >> 8) << 7) + (col & (LANES - 1))
            strict = key < t0 + _iota((tq, 2 * w2), 0)
            t = jnp.where(strict, t, 0.0)
        hi = t.astype(BF16)
        lo = (t - hi.astype(F32)).astype(BF16)
        lhs = jnp.concatenate(
            [jnp.concatenate([hi[:, g * LANES:(g + 1) * LANES], lo[:, g * LANES:(g + 1) * LANES]], axis=1)
             for g in range(4)], axis=0)
        r = _dot(lhs, tri)
        part = lambda g, c0: r[g * tq:(g + 1) * tq, c0:c0 + LANES]
        c_hi = c_ref[...]
        c_lo = c_hi + jnp.concatenate([part(2, LANES), part(3, LANES)], axis=1)
        rem = jnp.concatenate([jnp.concatenate([part(0, 0), part(1, 0)], axis=1) + c_lo,
                               jnp.concatenate([part(2, 0), part(3, 0)], axis=1) + c_hi], axis=1)
        a = jnp.exp2(log_beta + rem)
        if masked:
            a = jnp.where(strict, a, 0.0)
        c_ref[...] = c_lo + jnp.concatenate([part(0, LANES), part(1, LANES)], axis=1)
        return a.astype(BF16)

    def accumulate(a, kt):
        acc_ref[...] += _dot(a, vbd_ref[rows_of(kt), :])

    n_diag = sub // 2
    for d in range(n_diag):
        kt = (i + 1) * sub - 2 - 2 * d
        a = weights(scores(kt), kt, True)
        if d < n_diag - 1:
            accumulate(a, kt)
        else:
            a_ref[...] = a

    z_ref[...] = scores(jnp.maximum(i * sub - 2, 0))

    def body(j, carry):
        kt = i * sub - 2 - 2 * j
        z = z_ref[...]
        z_ref[...] = scores(jnp.maximum(kt - 2, 0))
        accumulate(a_ref[...], kt + 2)
        a_ref[...] = weights(z, kt, False)
        return carry

    lax.fori_loop(0, i * n_diag, body, 0)
    accumulate(a_ref[...], 0)
    o_ref[0] = acc_ref[...].astype(BF16)


def _stick_attention(q, k, v, tq=512):
    b, s, w = q.shape
    jj = np.arange(2 * LANES)[:, None] % LANES
    ss = np.arange(2 * LANES)[None, :]
    tri = jnp.asarray(-((ss >= LANES) | (jj > ss)).astype(np.float32), BF16)
    nt = s // LANES
    return pl.pallas_call(
        functools.partial(_stick_kernel, tq=tq),
        grid=(b, w // LANES, s // tq),
        in_specs=[pl.BlockSpec((1, tq, LANES), lambda bi, p, i: (bi, i, p)),
                  pl.BlockSpec((1, s, LANES), lambda bi, p, i: (bi, 0, p)),
                  pl.BlockSpec((1, s, LANES), lambda bi, p, i: (bi, 0, p)),
                  _const_spec(tri.shape)],
        out_specs=pl.BlockSpec((1, tq, LANES), lambda bi, p, i: (bi, i, p)),
        out_shape=jax.ShapeDtypeStruct((b, s, w), BF16),
        scratch_shapes=[pltpu.VMEM((nt * 2 * LANES, LANES), BF16),
                        pltpu.VMEM((nt * 2 * LANES, LANES), BF16),
                        pltpu.VMEM((tq, 2 * LANES), F32),
                        pltpu.VMEM((tq, LANES), F32),
                        pltpu.VMEM((tq, 4 * LANES), F32),
                        pltpu.VMEM((tq, 4 * LANES), BF16)],
        compiler_params=_params(("parallel", "parallel", "arbitrary")),
    )(q, k, v, tri)


def _outproj_ln_kernel(*refs, n_in):
    o_refs, w_refs = refs[:n_in], refs[n_in:2 * n_in]
    x_ref, g_ref, b_ref, out_ref = refs[2 * n_in:]
    m = _dot(o_refs[0][...], w_refs[0][...])
    for o_ref, w_ref in zip(o_refs[1:], w_refs[1:]):
        m = m + _dot(o_ref[...], w_ref[...])
    out_ref[...] = _layer_norm(DEEPNORM_ALPHA * x_ref[...] + m, g_ref[...], b_ref[...])


def _outproj_ln(os_, ws, x2, g, b, tm=512):
    n = x2.shape[0]
    n_in = len(os_)
    return pl.pallas_call(
        functools.partial(_outproj_ln_kernel, n_in=n_in),
        grid=(n // tm,),
        in_specs=([pl.BlockSpec((tm, o.shape[1]), lambda i: (i, 0)) for o in os_]
                  + [_const_spec(w.shape) for w in ws]
                  + [pl.BlockSpec((tm, D_MODEL), lambda i: (i, 0)),
                     _const_spec((1, D_MODEL)), _const_spec((1, D_MODEL))]),
        out_specs=pl.BlockSpec((tm, D_MODEL), lambda i: (i, 0)),
        out_shape=jax.ShapeDtypeStruct((n, D_MODEL), F32),
        compiler_params=_params(("parallel",)),
    )(*os_, *ws, x2, g.reshape(1, D_MODEL), b.reshape(1, D_MODEL))


def _ffn_ln_kernel(x_ref, wg_ref, wu_ref, wd_ref, g_ref, b_ref, out_ref, acc_ref, *, fc):
    x = x_ref[...]
    xb = x.astype(BF16)
    for c in range(D_FF // fc):
        gate = _dot(xb, wg_ref[:, c * fc:(c + 1) * fc])
        up = _dot(xb, wu_ref[:, c * fc:(c + 1) * fc])
        act = (gate * jax.nn.sigmoid(gate) * up).astype(BF16)
        d = _dot(act, wd_ref[c * fc:(c + 1) * fc, :])
        if c == 0:
            acc_ref[...] = d
        else:
            acc_ref[...] += d
    out_ref[...] = _layer_norm(DEEPNORM_ALPHA * x + acc_ref[...], g_ref[...], b_ref[...])


def _ffn_ln(x2, wg, wu, wd, g, b, tm=512, fc=256):
    n = x2.shape[0]
    single = dict(pipeline_mode=pl.Buffered(1))
    return pl.pallas_call(
        functools.partial(_ffn_ln_kernel, fc=fc),
        grid=(n // tm,),
        in_specs=[pl.BlockSpec((tm, D_MODEL), lambda i: (i, 0)),
                  pl.BlockSpec((D_MODEL, D_FF), lambda i: (0, 0), **single),
                  pl.BlockSpec((D_MODEL, D_FF), lambda i: (0, 0), **single),
                  pl.BlockSpec((D_FF, D_MODEL), lambda i: (0, 0), **single),
                  _const_spec((1, D_MODEL)), _const_spec((1, D_MODEL))],
        out_specs=pl.BlockSpec((tm, D_MODEL), lambda i: (i, 0)),
        out_shape=jax.ShapeDtypeStruct((n, D_MODEL), F32),
        scratch_shapes=[pltpu.VMEM((tm, D_MODEL), F32)],
        compiler_params=_params(("parallel",)),
    )(x2, wg.astype(BF16), wu.astype(BF16), wd.astype(BF16), g.reshape(1, D_MODEL), b.reshape(1, D_MODEL))


MOBA_SEL_LANE = 0
MOBA_POS_LANE = 32
MOBA_GROUP = 2


def _proj_odd_kernel(x_ref, w_ref, perm_ref, slope_ref, q_ref, k_ref, v_ref, qf_ref, kme_ref, *, nb):
    step = pl.program_id(0)
    n = step % nb
    nh = C_HEADS
    wq = nh * HEAD_DIM

    @pl.when(step == 0)
    def _():
        kme_ref[...] = jnp.zeros_like(kme_ref)

    xb = x_ref[...].astype(BF16)
    qf = _dot(xb, w_ref[:, 0:wq])
    kf = _dot(xb, w_ref[:, wq:2 * wq])
    q_ref[...] = (qf * HEAD_DIM ** -0.5).astype(BF16)
    k_ref[...] = kf.astype(BF16)
    v_ref[...] = _dot(xb, w_ref[:, 2 * wq:3 * wq]).astype(BF16)

    gate = _dot_nt(kme_ref[...], qf, precision=lax.Precision.HIGHEST)
    blk = _iota(gate.shape, 0) // nh
    gate = jnp.where(blk < n, gate, NEG)
    gs = [gate[nh * m:nh * (m + 1), :] for m in range(nb)]
    biases = []
    for a in range(nb):
        rank = jnp.zeros_like(gs[a])
        for m in range(nb):
            if m != a:
                beats = (gs[m] >= gs[a]) if m < a else (gs[m] > gs[a])
                rank = rank + jnp.where(beats, 1.0, 0.0)
        biases.append(jnp.where(rank < MOBA_TOPK, 0.0, NEG))
    bias = jnp.where(blk < n, jnp.concatenate(biases, axis=0), jnp.where(blk == n, 0.0, NEG))
    qf_ref[...] = (_dot(bias.T.astype(BF16), perm_ref[...]) + slope_ref[...]).astype(BF16)

    kmean = jnp.mean(kf, axis=0, keepdims=True)
    head = _iota(kmean.shape, 1) // HEAD_DIM
    for h in range(nh):
        kme_ref[pl.ds(n * nh + h, 1), :] = jnp.where(head == h, kmean, 0.0)


def _bf16_pieces(x):
    x = np.float32(x)
    out = []
    for _ in range(3):
        p = np.float32(np.asarray(x, dtype=BF16))
        out.append(p)
        x = np.float32(x - p)
    assert x == 0.0
    return out


def _moba_constants(nb, s):
    nh, p = C_HEADS, MOBA_BLOCK
    assert 2 * nb <= MOBA_POS_LANE and MOBA_POS_LANE + 12 <= LANES
    slopes = np.asarray(2.0 ** (-8.0 * np.arange(1, nh + 1) / nh), np.float32)
    perm = np.zeros((nb * nh, nh // 2 * LANES), np.float32)
    srow = np.zeros((1, nh // 2 * LANES), np.float32)
    for h in range(nh):
        base = (h // 2) * LANES
        for m in range(nb):
            perm[m * nh + h, base + MOBA_SEL_LANE + (h % 2) * nb + m] = 1.0
        pieces = _bf16_pieces(slopes[h])
        for c in range(6):
            srow[0, base + MOBA_POS_LANE + 6 * (h % 2) + c] = pieces[c % 3]
    pos = np.arange(s)
    kf = np.zeros((2, s, LANES), np.float32)
    for half in range(2):
        kf[half, pos, MOBA_SEL_LANE + half * nb + pos // p] = 1.0
        for c in range(3):
            kf[half, :, MOBA_POS_LANE + 6 * half + c] = pos % p
            kf[half, :, MOBA_POS_LANE + 6 * half + 3 + c] = p * (pos // p)
    vf = np.zeros((LANES, 2 * p), np.float32)
    vf[0, :p] = 1.0
    vf[1, p:] = 1.0
    return (jnp.asarray(perm, BF16), jnp.asarray(srow, F32), jnp.asarray(kf, BF16), jnp.asarray(vf, BF16))


def _proj_odd(x2, w_in, perm, srow, nb):
    n = x2.shape[0]
    p = MOBA_BLOCK
    wq = C_HEADS * HEAD_DIM
    w = w_in.astype(BF16)
    fw = perm.shape[1]
    return pl.pallas_call(
        functools.partial(_proj_odd_kernel, nb=nb),
        grid=(n // p,),
        in_specs=[pl.BlockSpec((p, D_MODEL), lambda i: (i, 0)), _const_spec(w.shape),
                  _const_spec(perm.shape), _const_spec(srow.shape)],
        out_specs=[pl.BlockSpec((p, wq), lambda i: (i, 0))] * 3 + [pl.BlockSpec((p, fw), lambda i: (i, 0))],
        out_shape=[jax.ShapeDtypeStruct((n, wq), BF16)] * 3 + [jax.ShapeDtypeStruct((n, fw), BF16)],
        scratch_shapes=[pltpu.VMEM((nb * C_HEADS, wq), F32)],
        compiler_params=_params(("arbitrary",)),
    )(x2, w, perm, srow)


def _moba_kernel(q_ref, qf_ref, k_ref, v_ref, kf_ref, vf_ref, o_ref, kbd_ref, vbt_ref, s_ref, *, nb):
    i = pl.program_id(2)
    p = MOBA_BLOCK

    @pl.when(i == 0)
    def _():
        lo = _iota((p, LANES), 1) < HEAD_DIM

        def build(n, carry):
            r0 = pl.multiple_of(n * p, p)
            d0 = pl.multiple_of(n * 2 * p, 2 * p)
            kk = k_ref[0, pl.ds(r0, p), :].astype(F32)
            vv = v_ref[0, pl.ds(r0, p), :].astype(F32)
            kbd_ref[pl.ds(d0, p), 0:LANES] = jnp.where(lo, kk, 0.0).astype(BF16)
            kbd_ref[pl.ds(d0, p), LANES:2 * LANES] = kf_ref[0, pl.ds(r0, p), :]
            kbd_ref[pl.ds(d0 + p, p), 0:LANES] = jnp.where(lo, 0.0, kk).astype(BF16)
            kbd_ref[pl.ds(d0 + p, p), LANES:2 * LANES] = kf_ref[1, pl.ds(r0, p), :]
            v_t = vv.T
            top = _iota(v_t.shape, 0) < HEAD_DIM
            vbt_ref[0:LANES, pl.ds(d0, p)] = jnp.where(top, v_t, 0.0).astype(BF16)
            vbt_ref[0:LANES, pl.ds(d0 + p, p)] = jnp.where(top, 0.0, v_t).astype(BF16)
            vbt_ref[LANES:2 * LANES, pl.ds(d0, 2 * p)] = vf_ref[...]
            return carry

        lax.fori_loop(0, nb, build, 0)

    tq = MOBA_GROUP * p
    gw = MOBA_GROUP * 2 * p
    lhs = jnp.concatenate([q_ref[0], qf_ref[0]], axis=1)

    def cols(g):
        return pl.ds(pl.multiple_of(g * gw, gw), gw)

    def scores(g):
        return _dot_nt(kbd_ref[cols(g), :], lhs)

    row_a = _iota((2 * LANES, tq), 0)
    head0 = (row_a < HEAD_DIM) | (row_a == LANES)

    def attend(g, s, carry, masked):
        m0, m1, acc = carry
        if masked:
            row = _iota((gw, tq), 0)
            key = ((row >> 9) << 8) + (row & (p - 1))
            s = jnp.where(key <= _iota((gw, tq), 1), s, NEG)
        s0 = [s[u * 2 * p:u * 2 * p + p] for u in range(MOBA_GROUP)]
        s1 = [s[u * 2 * p + p:(u + 1) * 2 * p] for u in range(MOBA_GROUP)]
        n0 = jnp.maximum(m0, jnp.max(functools.reduce(jnp.maximum, s0), axis=0, keepdims=True))
        n1 = jnp.maximum(m1, jnp.max(functools.reduce(jnp.maximum, s1), axis=0, keepdims=True))
        pexp = jnp.concatenate(
            [e for u in range(MOBA_GROUP) for e in (jnp.exp(s0[u] - n0), jnp.exp(s1[u] - n1))], axis=0)
        alpha = jnp.where(head0, jnp.exp(m0 - n0), jnp.exp(m1 - n1))
        return n0, n1, acc * alpha + _dot(vbt_ref[:, cols(g)], pexp.astype(BF16))

    s_ref[...] = scores(0)

    def body(g, carry):
        s = s_ref[...]
        s_ref[...] = scores(g + 1)
        return attend(g, s, carry, False)

    init = (jnp.full((1, tq), NEG, F32), jnp.full((1, tq), NEG, F32), jnp.zeros((2 * LANES, tq), F32))
    carry = lax.fori_loop(0, i, body, init)
    _, _, acc = attend(i, s_ref[...], carry, True)
    l = jnp.where(_iota((LANES, tq), 0) < HEAD_DIM, acc[LANES:LANES + 1, :], acc[LANES + 1:LANES + 2, :])
    o_ref[0] = (acc[:LANES, :] / l).T.astype(BF16)


def _moba_attention(q, qfeat, k, v, kf, vf):
    b, s, w = q.shape
    p = MOBA_BLOCK
    nb = s // p
    tq = MOBA_GROUP * p
    assert nb % MOBA_GROUP == 0
    return pl.pallas_call(
        functools.partial(_moba_kernel, nb=nb),
        grid=(b, w // LANES, s // tq),
        in_specs=[pl.BlockSpec((1, tq, LANES), lambda bi, hp, i: (bi, i, hp)),
                  pl.BlockSpec((1, tq, LANES), lambda bi, hp, i: (bi, i, hp)),
                  pl.BlockSpec((1, s, LANES), lambda bi, hp, i: (bi, 0, hp)),
                  pl.BlockSpec((1, s, LANES), lambda bi, hp, i: (bi, 0, hp)),
                  _const_spec(kf.shape), _const_spec(vf.shape)],
        out_specs=pl.BlockSpec((1, tq, LANES), lambda bi, hp, i: (bi, i, hp)),
        out_shape=jax.ShapeDtypeStruct((b, s, w), BF16),
        scratch_shapes=[pltpu.VMEM((nb * 2 * p, 2 * LANES), BF16),
                        pltpu.VMEM((2 * LANES, nb * 2 * p), BF16),
                        pltpu.VMEM((MOBA_GROUP * 2 * p, tq), F32)],
        compiler_params=_params(("parallel", "parallel", "arbitrary")),
    )(q, qfeat, k, v, kf, vf)


def kernel(x, even_w_in, even_kv_norm, even_w_uk, even_w_uv, even_w_out, odd_w_in, odd_w_out,
           ln_mix_g, ln_mix_b, ffn_gate, ffn_up, ffn_down, ln_ffn_g, ln_ffn_b):
    b, s, d = x.shape
    assert d == D_MODEL and s % (2 * MOBA_BLOCK) == 0 and s >= 4 * IDX_TOPK_MAX
    h = x.reshape(b * s, d)
    r3 = lambda t: t.reshape(b, s, t.shape[-1])
    for i in range(DEPTH):
        j = i // 2
        if i % 2 == 0:
            qlat, ckv, qidx, kidx, widx, qb, kb, vb = _proj_even(h, even_w_in[j], even_kv_norm[j], even_w_uk[j])
            o_a = _dsa_attention(r3(qlat), r3(qidx), r3(widx), r3(kidx), r3(ckv), even_w_uv[j])
            o_b = _stick_attention(r3(qb), r3(kb), r3(vb))
            w_out = even_w_out[j].astype(BF16)
            na = A_HEADS * HEAD_DIM
            outs = [o_a.reshape(b * s, -1), o_b.reshape(b * s, -1)]
            ws = [w_out[:na], w_out[na:]]
        else:
            nb = s // MOBA_BLOCK
            perm, srow, kf, vf = _moba_constants(nb, s)
            q, k, v, qfeat = _proj_odd(h, odd_w_in[j], perm, srow, nb)
            o_c = _moba_attention(r3(q), r3(qfeat), r3(k), r3(v), kf, vf)
            outs = [o_c.reshape(b * s, -1)]
            ws = [odd_w_out[j].astype(BF16)]
        h = _outproj_ln(outs, ws, h, ln_mix_g[i], ln_mix_b[i])
        h = _ffn_ln(h, ffn_gate[i], ffn_up[i], ffn_down[i], ln_ffn_g[i], ln_ffn_b[i])
    return h.reshape(b, s, d)
```

```python
import functools

import numpy as np
import jax
import jax.numpy as jnp
from jax import lax
from jax.experimental import pallas as pl
from jax.experimental.pallas import tpu as pltpu

D_MODEL = 1024
DEPTH = 2
HEAD_DIM = 64
A_HEADS = 8
A_KV_RANK = 128
IDX_HEADS = 4
IDX_DIM = 64
IDX_TOPK_MAX = 256
B_HEADS = 8
C_HEADS = 16
MOBA_BLOCK = 256
MOBA_TOPK = 3
D_FF = -(-8 * D_MODEL // (3 * 256)) * 256
DEEPNORM_ALPHA = (2 * DEPTH) ** 0.25
NEG = -1e30
LOG2E = 1.4426950408889634
SIGN_BIT = -2 ** 31

LANES = 128
VMEM_LIMIT = 52 * 1024 * 1024

BF16 = jnp.bfloat16
F32 = jnp.float32
I32 = jnp.int32


def _dot(a, b):
    return jnp.dot(a, b, preferred_element_type=F32)


def _dot_nt(a, b, precision=None):
    return lax.dot_general(a, b, (((1,), (1,)), ((), ())), precision=precision,
                           preferred_element_type=F32)


def _split_bf16(x):
    hi = x.astype(BF16)
    return hi, (x - hi.astype(F32)).astype(BF16)


def _iota(shape, dim):
    return lax.broadcasted_iota(I32, shape, dim)


def _params(sem, vmem=VMEM_LIMIT):
    return pltpu.CompilerParams(dimension_semantics=sem, vmem_limit_bytes=vmem)


def _const_spec(shape):
    nd = len(shape)
    return pl.BlockSpec(shape, lambda *_: (0,) * nd)


def _layer_norm(y, g, b):
    mu = jnp.mean(y, axis=-1, keepdims=True)
    d = y - mu
    var = jnp.mean(d * d, axis=-1, keepdims=True)
    return d * lax.rsqrt(var + 1e-5) * g + b


def _proj_even_kernel(x_ref, w_ref, wuk_ref, g_ref,
                      qlat_ref, ckv_ref, qidx_ref, kidx_ref, widx_ref, qb_ref, kb_ref, vb_ref):
    xb = x_ref[...].astype(BF16)
    qa = _dot(xb, w_ref[:, 0:512]).astype(BF16)
    for p in range(A_HEADS // 2):
        ql = _dot(qa[:, 128 * p:128 * (p + 1)], wuk_ref[p])
        qlat_ref[:, 256 * p:256 * (p + 1)] = (ql * HEAD_DIM ** -0.5).astype(BF16)
    c = _dot(xb, w_ref[:, 512:640])
    ms = jnp.mean(c * c, axis=-1, keepdims=True)
    ckv_ref[...] = (c * lax.rsqrt(ms + 1e-6) * g_ref[...]).astype(BF16)
    qidx_ref[...] = (_dot(xb, w_ref[:, 640:896]) * IDX_DIM ** -0.5).astype(BF16)
    kidx_ref[...] = _dot(xb, w_ref[:, 896:1024]).astype(BF16)
    widx_ref[...] = _dot(xb, w_ref[:, 1024:1152]) * IDX_HEADS ** -0.5
    qb_ref[...] = (_dot(xb, w_ref[:, 1152:1664]) * HEAD_DIM ** -0.5).astype(BF16)
    kb_ref[...] = _dot(xb, w_ref[:, 1664:2176]).astype(BF16)
    vb_ref[...] = _dot(xb, w_ref[:, 2176:2688]).astype(BF16)


def _proj_even(x2, w_in, kv_norm, w_uk, tm=512):
    n = x2.shape[0]
    cuts = np.cumsum([512, 128, 256, 64, 4, 512, 512, 512])[:-1].tolist()
    q_a, c_kv, q_idx, k_idx, w_idx, q_b, k_b, v_b = jnp.split(w_in, cuts, axis=1)
    w_idx = jnp.pad(w_idx, ((0, 0), (0, LANES - IDX_HEADS)))
    w = jnp.concatenate([q_a, c_kv, q_idx, k_idx, k_idx, w_idx, q_b, k_b, v_b], axis=1).astype(BF16)
    z = jnp.zeros((HEAD_DIM, A_KV_RANK), F32)
    wuk = jnp.stack([jnp.block([[w_uk[2 * p], z], [z, w_uk[2 * p + 1]]]) for p in range(A_HEADS // 2)]).astype(BF16)
    g = kv_norm.reshape(1, A_KV_RANK)
    widths = [(1024, BF16), (128, BF16), (256, BF16), (128, BF16), (128, F32), (512, BF16), (512, BF16), (512, BF16)]
    return pl.pallas_call(
        _proj_even_kernel,
        grid=(n // tm,),
        in_specs=[pl.BlockSpec((tm, D_MODEL), lambda i: (i, 0)),
                  _const_spec(w.shape), _const_spec(wuk.shape), _const_spec(g.shape)],
        out_specs=[pl.BlockSpec((tm, wd), lambda i: (i, 0)) for wd, _ in widths],
        out_shape=[jax.ShapeDtypeStruct((n, wd), dt) for wd, dt in widths],
        compiler_params=_params(("parallel",)),
    )(x2, w, wuk, g)


def _dsa_kernel(qlat_ref, qidx_ref, widx_ref, kidx_ref, ckv_ref, kfeat_ref, wuv_ref, o_ref,
                key_ref, msk_ref, ckv2_ref, logit_ref, mrun_ref, acc_ref, olat_ref, *, tq, topk):
    i = pl.program_id(1)
    t0 = i * tq
    tk = tq
    hq = tq // 2
    nt = key_ref.shape[0]
    n_pair = (i + 2) // 2
    int_min = -2 ** 31

    @pl.when(i == 0)
    def _():
        ckv2_ref[:, 0:LANES] = ckv_ref[0]
        ckv2_ref[:, LANES:2 * LANES] = kfeat_ref[...]

    rel = _iota((tk, tq), 0) - _iota((tk, tq), 1)
    lane_q = _iota((tq, LANES), 1)

    qm = []
    for h in range(IDX_HEADS):
        p, half = divmod(h, 2)
        qi = qidx_ref[0, :, 128 * p:128 * (p + 1)].astype(F32)
        qm.append(jnp.where((lane_q < IDX_DIM) == (half == 0), qi, 0.0).astype(BF16))
    w_t = widx_ref[0].T
    w_row = [w_t[h:h + 1, :] for h in range(IDX_HEADS)]

    def score_body(kp, carry):
        for u in range(2):
            kt = 2 * kp + u
            k0 = pl.multiple_of(kt * tk, tk)
            k2 = kidx_ref[0, pl.ds(k0, tk), :]
            acc = w_row[0] * jnp.maximum(_dot_nt(k2, qm[0]), 0.0)
            for h in range(1, IDX_HEADS):
                acc = acc + w_row[h] * jnp.maximum(_dot_nt(k2, qm[h]), 0.0)
            acc = jnp.where(rel <= t0 - k0, acc, NEG)
            bits = lax.bitcast_convert_type(acc, I32)
            key_ref[kt] = bits ^ ((bits >> 31) & 0x7FFFFFFF)
        return carry

    lax.fori_loop(0, n_pair, score_body, 0)

    fold = 64

    def bit_body(bi, carry):
        cur, cfail = carry
        cand = cur + lax.shift_left(jnp.int32(1), 31 - bi)

        def cnt_body(kp, acc):
            for u in range(2):
                hit = jnp.where(key_ref[2 * kp + u] >= cand, 1.0, 0.0)
                acc = acc + jnp.sum(hit.reshape(tk // fold, fold, tq), axis=0)
            return acc

        acc = lax.fori_loop(0, n_pair, cnt_body, jnp.zeros((fold, tq), F32))
        cnt = jnp.sum(acc, axis=0, keepdims=True)
        ok = cnt >= topk
        return jnp.where(ok, cand, cur), jnp.where(ok, cfail, cnt)

    thr, cnt_gt = lax.fori_loop(
        0, 32, bit_body, (jnp.full((1, tq), int_min, I32), jnp.zeros((1, tq), F32)))
    need = topk - cnt_gt

    ltri = jnp.where(_iota((tk, tk), 0) >= _iota((tk, tk), 1), 1.0, 0.0).astype(BF16)

    def mask_body(kp, seen):
        for u in range(2):
            kt = 2 * kp + u
            kk = key_ref[kt]
            eq = kk == thr
            pc = _dot(ltri, jnp.where(eq, 1.0, 0.0).astype(BF16))
            take = eq & (pc + seen <= need)
            sel = ((kk > thr) | take) & (rel <= t0 - kt * tk)
            msk_ref[kt] = jnp.where(sel, 0.0, NEG).T
            seen = seen + pc[tk - 1:tk, :]
        return seen

    lax.fori_loop(0, n_pair, mask_body, jnp.zeros((1, tq), F32))

    lane_h = _iota((hq, LANES), 1)
    for half in range(2):
        r0, r1 = half * hq, (half + 1) * hq
        t0h = (t0 + r0).astype(F32)
        parts = []
        for h in range(A_HEADS):
            slope = 2.0 ** (-8.0 * (h + 1) / A_HEADS)
            qf = jnp.where(lane_h < 2, slope, jnp.where(lane_h == 2, -slope * t0h, 0.0)).astype(BF16)
            parts.append(jnp.concatenate([qlat_ref[0, r0:r1, 128 * h:128 * (h + 1)], qf], axis=1))
        lhs = jnp.concatenate(parts, axis=0)
        mrun_ref[...] = jnp.full(mrun_ref.shape, NEG, F32)

        def pass1(kp, carry, lhs=lhs, r0=r0, r1=r1):
            k0 = pl.multiple_of(kp * 2 * tk, 2 * tk)
            s = _dot_nt(lhs, ckv2_ref[pl.ds(k0, 2 * tk), :])
            mk = [msk_ref[2 * kp + u, r0:r1, :] for u in range(2)]
            for h in range(A_HEADS):
                top = mrun_ref[h * hq:(h + 1) * hq, :]
                for u in range(2):
                    sh = s[h * hq:(h + 1) * hq, u * tk:(u + 1) * tk] + mk[u]
                    logit_ref[2 * kp + u, h * hq:(h + 1) * hq, :] = sh
                    top = jnp.maximum(top, jnp.maximum(sh[:, :LANES], sh[:, LANES:]))
                mrun_ref[h * hq:(h + 1) * hq, :] = top
            return carry

        lax.fori_loop(0, n_pair, pass1, 0)
        m = jnp.max(mrun_ref[...], axis=-1, keepdims=True)
        acc_ref[...] = jnp.zeros(acc_ref.shape, F32)

        def pass2(kp, carry, m=m):
            p0 = jnp.exp(logit_ref[2 * kp] - m).astype(BF16)
            p1 = jnp.exp(logit_ref[2 * kp + 1] - m).astype(BF16)
            k0 = pl.multiple_of(kp * 2 * tk, 2 * tk)
            acc_ref[...] += _dot(jnp.concatenate([p0, p1], axis=1), ckv2_ref[pl.ds(k0, 2 * tk), :])
            return carry

        lax.fori_loop(0, n_pair, pass2, 0)
        acc = acc_ref[...]
        o = acc[:, :LANES] / acc[:, LANES + 2:LANES + 3]
        for h in range(A_HEADS):
            olat_ref[r0:r1, 128 * h:128 * (h + 1)] = o[h * hq:(h + 1) * hq]

    for p in range(A_HEADS // 2):
        o = _dot(olat_ref[:, 256 * p:256 * (p + 1)].astype(BF16), wuv_ref[p])
        o_ref[0, :, 128 * p:128 * (p + 1)] = o.astype(BF16)


def _dsa_attention(qlat, qidx, widx, kidx, ckv, w_uv, tq=256):
    b, s, _ = qlat.shape
    topk = min(IDX_TOPK_MAX, s // 4)
    assert topk <= tq and s % (2 * tq) == 0
    z = jnp.zeros((A_KV_RANK, HEAD_DIM), F32)
    wuv = jnp.stack([jnp.block([[w_uv[2 * p], z], [z, w_uv[2 * p + 1]]]) for p in range(A_HEADS // 2)]).astype(BF16)
    pos = np.arange(s)
    kfeat = np.zeros((s, LANES), np.float32)
    kfeat[:, 0], kfeat[:, 1], kfeat[:, 2] = 256 * (pos // 256), pos % 256, 1.0
    kfeat = jnp.asarray(kfeat, BF16)
    nt = s // tq
    rows = A_HEADS * tq // 2
    return pl.pallas_call(
        functools.partial(_dsa_kernel, tq=tq, topk=topk),
        grid=(b, s // tq),
        in_specs=[pl.BlockSpec((1, tq, A_HEADS * A_KV_RANK), lambda bi, i: (bi, i, 0)),
                  pl.BlockSpec((1, tq, IDX_HEADS * IDX_DIM), lambda bi, i: (bi, i, 0)),
                  pl.BlockSpec((1, tq, LANES), lambda bi, i: (bi, i, 0)),
                  pl.BlockSpec((1, s, LANES), lambda bi, i: (bi, 0, 0)),
                  pl.BlockSpec((1, s, A_KV_RANK), lambda bi, i: (bi, 0, 0)),
                  _const_spec(kfeat.shape), _const_spec(wuv.shape)],
        out_specs=pl.BlockSpec((1, tq, A_HEADS * HEAD_DIM), lambda bi, i: (bi, i, 0)),
        out_shape=jax.ShapeDtypeStruct((b, s, A_HEADS * HEAD_DIM), BF16),
        scratch_shapes=[pltpu.VMEM((nt, tq, tq), I32),
                        pltpu.VMEM((nt, tq, tq), F32),
                        pltpu.VMEM((s, 2 * LANES), BF16),
                        pltpu.VMEM((nt, rows, tq), F32),
                        pltpu.VMEM((rows, LANES), F32),
                        pltpu.VMEM((rows, 2 * LANES), F32),
                        pltpu.VMEM((tq, A_HEADS * A_KV_RANK), F32)],
        compiler_params=_params(("parallel", "arbitrary")),
    )(qlat, qidx, widx, kidx, ckv, kfeat, wuv)


def _stick_kernel(q_ref, k_ref, v_ref, tri_ref, o_ref, kbd_ref, vbd_ref, c_ref, acc_ref, z_ref, a_ref, *, tq):
    i = pl.program_id(2)
    t0 = i * tq
    w2 = 2 * LANES
    nt = kbd_ref.shape[0] // w2
    sub = tq // LANES
    assert sub % 2 == 0

    @pl.when(i == 0)
    def _():
        lo = _iota((LANES, LANES), 1) < HEAD_DIM

        def build(kt, carry):
            r0 = pl.multiple_of(kt * LANES, LANES)
            d0 = pl.multiple_of(kt * w2, w2)
            kk = k_ref[0, pl.ds(r0, LANES), :].astype(F32)
            vv = v_ref[0, pl.ds(r0, LANES), :].astype(F32)
            kbd_ref[pl.ds(d0, LANES), :] = jnp.where(lo, kk, 0.0).astype(BF16)
            kbd_ref[pl.ds(d0 + LANES, LANES), :] = jnp.where(lo, 0.0, kk).astype(BF16)
            vbd_ref[pl.ds(d0, LANES), :] = jnp.where(lo, vv, 0.0).astype(BF16)
            vbd_ref[pl.ds(d0 + LANES, LANES), :] = jnp.where(lo, 0.0, vv).astype(BF16)
            return carry

        lax.fori_loop(0, nt, build, 0)

    q = q_ref[0]
    tri = tri_ref[...]
    c_ref[...] = jnp.zeros(c_ref.shape, F32)
    acc_ref[...] = jnp.zeros(acc_ref.shape, F32)

    def rows_of(kt):
        start = kt * w2
        return pl.ds(start if isinstance(start, int) else pl.multiple_of(start, 2 * w2), 2 * w2)

    def scores(kt, r0=0):
        return _dot_nt(q[r0:], kbd_ref[rows_of(kt), :])

    def weights(z, kt, masked, r0=0):
        nr = tq - r0
        z = z * LOG2E
        neg_abs = lax.bitcast_convert_type(lax.bitcast_convert_type(z, I32) | SIGN_BIT, F32)
        t = jnp.maximum(z, 0.0) + jnp.log2(1.0 + jnp.exp2(neg_abs))
        log_beta = z - t
        if masked:
            col = _iota((nr, 2 * w2), 1)
            key = kt * LANES + ((col >> 8) << 7) + (col & (LANES - 1))
            strict = key < t0 + r0 + _iota((nr, 2 * w2), 0)
            t = jnp.where(strict, t, 0.0)
        hi = t.astype(BF16)
        lo = (t - hi.astype(F32)).astype(BF16)
        lhs = jnp.concatenate(
            [jnp.concatenate([hi[:, g * LANES:(g + 1) * LANES], lo[:, g * LANES:(g + 1) * LANES]], axis=1)
             for g in range(4)], axis=0)
        r = _dot(lhs, tri)
        part = lambda g, c0: r[g * nr:(g + 1) * nr, c0:c0 + LANES]
        c_hi = c_ref[r0:, :]
        c_lo = c_hi + jnp.concatenate([part(2, LANES), part(3, LANES)], axis=1)
        rem = jnp.concatenate([jnp.concatenate([part(0, 0), part(1, 0)], axis=1) + c_lo,
                               jnp.concatenate([part(2, 0), part(3, 0)], axis=1) + c_hi], axis=1)
        a = jnp.exp2(log_beta + rem)
        if masked:
            a = jnp.where(strict, a, 0.0)
        c_ref[r0:, :] = c_lo + jnp.concatenate([part(0, LANES), part(1, LANES)], axis=1)
        return a.astype(BF16)

    def accumulate(a, kt, r0=0):
        acc_ref[r0:, :] += _dot(a, vbd_ref[rows_of(kt), :])

    n_diag = sub // 2
    for d in range(n_diag):
        kt = (i + 1) * sub - 2 - 2 * d
        r0 = (sub - 2 - 2 * d) * LANES
        a = weights(scores(kt, r0), kt, True, r0)
        if d < n_diag - 1:
            accumulate(a, kt, r0)
        else:
            a_ref[...] = a

    z_ref[...] = scores(jnp.maximum(i * sub - 2, 0))

    def body(j, carry):
        kt = i * sub - 2 - 2 * j
        z = z_ref[...]
        z_ref[...] = scores(jnp.maximum(kt - 2, 0))
        accumulate(a_ref[...], kt + 2)
        a_ref[...] = weights(z, kt, False)
        return carry

    lax.fori_loop(0, i * n_diag, body, 0)
    accumulate(a_ref[...], 0)
    o_ref[0] = acc_ref[...].astype(BF16)


def _stick_attention(q, k, v, tq=512):
    b, s, w = q.shape
    jj = np.arange(2 * LANES)[:, None] % LANES
    ss = np.arange(2 * LANES)[None, :]
    tri = jnp.asarray(-((ss >= LANES) | (jj > ss)).astype(np.float32), BF16)
    nt = s // LANES
    return pl.pallas_call(
        functools.partial(_stick_kernel, tq=tq),
        grid=(b, w // LANES, s // tq),
        in_specs=[pl.BlockSpec((1, tq, LANES), lambda bi, p, i: (bi, i, p)),
                  pl.BlockSpec((1, s, LANES), lambda bi, p, i: (bi, 0, p)),
                  pl.BlockSpec((1, s, LANES), lambda bi, p, i: (bi, 0, p)),
                  _const_spec(tri.shape)],
        out_specs=pl.BlockSpec((1, tq, LANES), lambda bi, p, i: (bi, i, p)),
        out_shape=jax.ShapeDtypeStruct((b, s, w), BF16),
        scratch_shapes=[pltpu.VMEM((nt * 2 * LANES, LANES), BF16),
                        pltpu.VMEM((nt * 2 * LANES, LANES), BF16),
                        pltpu.VMEM((tq, 2 * LANES), F32),
                        pltpu.VMEM((tq, LANES), F32),
                        pltpu.VMEM((tq, 4 * LANES), F32),
                        pltpu.VMEM((tq, 4 * LANES), BF16)],
        compiler_params=_params(("parallel", "parallel", "arbitrary")),
    )(q, k, v, tri)


def _layer_tail_kernel(*refs, n_in, fc):
    o_refs, w_refs = refs[:n_in], refs[n_in:2 * n_in]
    x_ref, g1_ref, b1_ref, wg_ref, wu_ref, wd_ref, g2_ref, b2_ref, out_ref, acc_ref = refs[2 * n_in:]
    m = _dot(o_refs[0][...], w_refs[0][...])
    for o_ref, w_ref in zip(o_refs[1:], w_refs[1:]):
        m = m + _dot(o_ref[...], w_ref[...])
    h = _layer_norm(DEEPNORM_ALPHA * x_ref[...] + m, g1_ref[...], b1_ref[...])
    hb = h.astype(BF16)
    for c in range(D_FF // fc):
        gate = _dot(hb, wg_ref[:, c * fc:(c + 1) * fc])
        up = _dot(hb, wu_ref[:, c * fc:(c + 1) * fc])
        act = (gate * jax.nn.sigmoid(gate) * up).astype(BF16)
        d = _dot(act, wd_ref[c * fc:(c + 1) * fc, :])
        if c == 0:
            acc_ref[...] = d
        else:
            acc_ref[...] += d
    out_ref[...] = _layer_norm(DEEPNORM_ALPHA * h + acc_ref[...], g2_ref[...], b2_ref[...])


def _layer_tail(os_, ws, x2, g1, b1, wg, wu, wd, g2, b2, tm=512, fc=256):
    n = x2.shape[0]
    n_in = len(os_)
    single = dict(pipeline_mode=pl.Buffered(1))
    vec = lambda v: v.reshape(1, D_MODEL)
    return pl.pallas_call(
        functools.partial(_layer_tail_kernel, n_in=n_in, fc=fc),
        grid=(n // tm,),
        in_specs=([pl.BlockSpec((tm, o.shape[1]), lambda i: (i, 0)) for o in os_]
                  + [pl.BlockSpec(w.shape, lambda i: (0, 0), **single) for w in ws]
                  + [pl.BlockSpec((tm, D_MODEL), lambda i: (i, 0)),
                     _const_spec((1, D_MODEL)), _const_spec((1, D_MODEL)),
                     pl.BlockSpec((D_MODEL, D_FF), lambda i: (0, 0), **single),
                     pl.BlockSpec((D_MODEL, D_FF), lambda i: (0, 0), **single),
                     pl.BlockSpec((D_FF, D_MODEL), lambda i: (0, 0), **single),
                     _const_spec((1, D_MODEL)), _const_spec((1, D_MODEL))]),
        out_specs=pl.BlockSpec((tm, D_MODEL), lambda i: (i, 0)),
        out_shape=jax.ShapeDtypeStruct((n, D_MODEL), F32),
        scratch_shapes=[pltpu.VMEM((tm, D_MODEL), F32)],
        compiler_params=_params(("parallel",)),
    )(*os_, *ws, x2, vec(g1), vec(b1), wg.astype(BF16), wu.astype(BF16), wd.astype(BF16), vec(g2), vec(b2))


MOBA_SEL_LANE = 0
MOBA_POS_LANE = 32
MOBA_GROUP = 2


def _proj_odd_kernel(x_ref, w_ref, perm_ref, slope_ref, q_ref, k_ref, v_ref, qf_ref, kme_ref, *, nb):
    step = pl.program_id(0)
    n = step % nb
    nh = C_HEADS
    wq = nh * HEAD_DIM

    @pl.when(step == 0)
    def _():
        kme_ref[...] = jnp.zeros_like(kme_ref)

    xb = x_ref[...].astype(BF16)
    qf = _dot(xb, w_ref[:, 0:wq])
    kf = _dot(xb, w_ref[:, wq:2 * wq])
    q_ref[...] = (qf * HEAD_DIM ** -0.5).astype(BF16)
    k_ref[...] = kf.astype(BF16)
    v_ref[...] = _dot(xb, w_ref[:, 2 * wq:3 * wq]).astype(BF16)

    k_hi, k_lo = _split_bf16(kme_ref[...])
    q_hi, q_lo = _split_bf16(qf)
    gate = _dot_nt(k_hi, q_hi) + _dot_nt(k_hi, q_lo) + _dot_nt(k_lo, q_hi)
    blk = _iota(gate.shape, 0) // nh
    gate = jnp.where(blk < n, gate, NEG)
    gs = [gate[nh * m:nh * (m + 1), :] for m in range(nb)]
    biases = []
    for a in range(nb):
        rank = jnp.zeros_like(gs[a])
        for m in range(nb):
            if m != a:
                beats = (gs[m] >= gs[a]) if m < a else (gs[m] > gs[a])
                rank = rank + jnp.where(beats, 1.0, 0.0)
        biases.append(jnp.where(rank < MOBA_TOPK, 0.0, NEG))
    bias = jnp.where(blk < n, jnp.concatenate(biases, axis=0), jnp.where(blk == n, 0.0, NEG))
    qf_ref[...] = (_dot(bias.T.astype(BF16), perm_ref[...]) + slope_ref[...]).astype(BF16)

    kmean = jnp.mean(kf, axis=0, keepdims=True)
    head = _iota(kmean.shape, 1) // HEAD_DIM
    for h in range(nh):
        kme_ref[pl.ds(n * nh + h, 1), :] = jnp.where(head == h, kmean, 0.0)


def _bf16_pieces(x):
    x = np.float32(x)
    out = []
    for _ in range(3):
        p = np.float32(np.asarray(x, dtype=BF16))
        out.append(p)
        x = np.float32(x - p)
    assert x == 0.0
    return out


def _moba_constants(nb, s):
    nh, p = C_HEADS, MOBA_BLOCK
    assert 2 * nb <= MOBA_POS_LANE and MOBA_POS_LANE + 12 <= LANES
    slopes = np.asarray(2.0 ** (-8.0 * np.arange(1, nh + 1) / nh), np.float32)
    perm = np.zeros((nb * nh, nh // 2 * LANES), np.float32)
    srow = np.zeros((1, nh // 2 * LANES), np.float32)
    for h in range(nh):
        base = (h // 2) * LANES
        for m in range(nb):
            perm[m * nh + h, base + MOBA_SEL_LANE + (h % 2) * nb + m] = 1.0
        pieces = _bf16_pieces(slopes[h])
        for c in range(6):
            srow[0, base + MOBA_POS_LANE + 6 * (h % 2) + c] = pieces[c % 3]
    pos = np.arange(s)
    kf = np.zeros((2, s, LANES), np.float32)
    for half in range(2):
        kf[half, pos, MOBA_SEL_LANE + half * nb + pos // p] = 1.0
        for c in range(3):
            kf[half, :, MOBA_POS_LANE + 6 * half + c] = pos % p
            kf[half, :, MOBA_POS_LANE + 6 * half + 3 + c] = p * (pos // p)
    vf = np.zeros((LANES, 2 * p), np.float32)
    vf[0, :p] = 1.0
    vf[1, p:] = 1.0
    return (jnp.asarray(perm, BF16), jnp.asarray(srow, F32), jnp.asarray(kf, BF16), jnp.asarray(vf, BF16))


def _proj_odd(x2, w_in, perm, srow, nb):
    n = x2.shape[0]
    p = MOBA_BLOCK
    wq = C_HEADS * HEAD_DIM
    w = w_in.astype(BF16)
    fw = perm.shape[1]
    return pl.pallas_call(
        functools.partial(_proj_odd_kernel, nb=nb),
        grid=(n // p,),
        in_specs=[pl.BlockSpec((p, D_MODEL), lambda i: (i, 0)), _const_spec(w.shape),
                  _const_spec(perm.shape), _const_spec(srow.shape)],
        out_specs=[pl.BlockSpec((p, wq), lambda i: (i, 0))] * 3 + [pl.BlockSpec((p, fw), lambda i: (i, 0))],
        out_shape=[jax.ShapeDtypeStruct((n, wq), BF16)] * 3 + [jax.ShapeDtypeStruct((n, fw), BF16)],
        scratch_shapes=[pltpu.VMEM((nb * C_HEADS, wq), F32)],
        compiler_params=_params(("arbitrary",)),
    )(x2, w, perm, srow)


def _moba_kernel(q_ref, qf_ref, k_ref, v_ref, kf_ref, vf_ref, o_ref, kbd_ref, vbt_ref, s_ref, *, nb):
    i = pl.program_id(2)
    p = MOBA_BLOCK

    @pl.when(i == 0)
    def _():
        lo = _iota((p, LANES), 1) < HEAD_DIM

        def build(n, carry):
            r0 = pl.multiple_of(n * p, p)
            d0 = pl.multiple_of(n * 2 * p, 2 * p)
            kk = k_ref[0, pl.ds(r0, p), :].astype(F32)
            vv = v_ref[0, pl.ds(r0, p), :].astype(F32)
            kbd_ref[pl.ds(d0, p), 0:LANES] = jnp.where(lo, kk, 0.0).astype(BF16)
            kbd_ref[pl.ds(d0, p), LANES:2 * LANES] = kf_ref[0, pl.ds(r0, p), :]
            kbd_ref[pl.ds(d0 + p, p), 0:LANES] = jnp.where(lo, 0.0, kk).astype(BF16)
            kbd_ref[pl.ds(d0 + p, p), LANES:2 * LANES] = kf_ref[1, pl.ds(r0, p), :]
            v_t = vv.T
            top = _iota(v_t.shape, 0) < HEAD_DIM
            vbt_ref[0:LANES, pl.ds(d0, p)] = jnp.where(top, v_t, 0.0).astype(BF16)
            vbt_ref[0:LANES, pl.ds(d0 + p, p)] = jnp.where(top, 0.0, v_t).astype(BF16)
            vbt_ref[LANES:2 * LANES, pl.ds(d0, 2 * p)] = vf_ref[...]
            return carry

        lax.fori_loop(0, nb, build, 0)

    tq = MOBA_GROUP * p
    gw = MOBA_GROUP * 2 * p
    lhs = jnp.concatenate([q_ref[0], qf_ref[0]], axis=1)

    def cols(g):
        return pl.ds(pl.multiple_of(g * gw, gw), gw)

    def scores(g):
        return _dot_nt(kbd_ref[cols(g), :], lhs)

    row_a = _iota((2 * LANES, tq), 0)
    head0 = (row_a < HEAD_DIM) | (row_a == LANES)

    def attend(g, s, carry, masked):
        m0, m1, acc = carry
        if masked:
            row = _iota((gw, tq), 0)
            key = ((row >> 9) << 8) + (row & (p - 1))
            s = jnp.where(key <= _iota((gw, tq), 1), s, NEG)
        s0 = [s[u * 2 * p:u * 2 * p + p] for u in range(MOBA_GROUP)]
        s1 = [s[u * 2 * p + p:(u + 1) * 2 * p] for u in range(MOBA_GROUP)]
        n0 = jnp.maximum(m0, jnp.max(functools.reduce(jnp.maximum, s0), axis=0, keepdims=True))
        n1 = jnp.maximum(m1, jnp.max(functools.reduce(jnp.maximum, s1), axis=0, keepdims=True))
        pexp = jnp.concatenate(
            [e for u in range(MOBA_GROUP) for e in (jnp.exp(s0[u] - n0), jnp.exp(s1[u] - n1))], axis=0)
        alpha = jnp.where(head0, jnp.exp(m0 - n0), jnp.exp(m1 - n1))
        return n0, n1, acc * alpha + _dot(vbt_ref[:, cols(g)], pexp.astype(BF16))

    s_ref[...] = scores(0)

    def body(g, carry):
        s = s_ref[...]
        s_ref[...] = scores(g + 1)
        return attend(g, s, carry, False)

    init = (jnp.full((1, tq), NEG, F32), jnp.full((1, tq), NEG, F32), jnp.zeros((2 * LANES, tq), F32))
    carry = lax.fori_loop(0, i, body, init)
    _, _, acc = attend(i, s_ref[...], carry, True)
    l = jnp.where(_iota((LANES, tq), 0) < HEAD_DIM, acc[LANES:LANES + 1, :], acc[LANES + 1:LANES + 2, :])
    o_ref[0] = (acc[:LANES, :] / l).T.astype(BF16)


def _moba_attention(q, qfeat, k, v, kf, vf):
    b, s, w = q.shape
    p = MOBA_BLOCK
    nb = s // p
    tq = MOBA_GROUP * p
    assert nb % MOBA_GROUP == 0
    return pl.pallas_call(
        functools.partial(_moba_kernel, nb=nb),
        grid=(b, w // LANES, s // tq),
        in_specs=[pl.BlockSpec((1, tq, LANES), lambda bi, hp, i: (bi, i, hp)),
                  pl.BlockSpec((1, tq, LANES), lambda bi, hp, i: (bi, i, hp)),
                  pl.BlockSpec((1, s, LANES), lambda bi, hp, i: (bi, 0, hp)),
                  pl.BlockSpec((1, s, LANES), lambda bi, hp, i: (bi, 0, hp)),
                  _const_spec(kf.shape), _const_spec(vf.shape)],
        out_specs=pl.BlockSpec((1, tq, LANES), lambda bi, hp, i: (bi, i, hp)),
        out_shape=jax.ShapeDtypeStruct((b, s, w), BF16),
        scratch_shapes=[pltpu.VMEM((nb * 2 * p, 2 * LANES), BF16),
                        pltpu.VMEM((2 * LANES, nb * 2 * p), BF16),
                        pltpu.VMEM((MOBA_GROUP * 2 * p, tq), F32)],
        compiler_params=_params(("parallel", "parallel", "arbitrary")),
    )(q, qfeat, k, v, kf, vf)


def kernel(x, even_w_in, even_kv_norm, even_w_uk, even_w_uv, even_w_out, odd_w_in, odd_w_out,
           ln_mix_g, ln_mix_b, ffn_gate, ffn_up, ffn_down, ln_ffn_g, ln_ffn_b):
    b, s, d = x.shape
    assert d == D_MODEL and s % (2 * MOBA_BLOCK) == 0 and s >= 4 * IDX_TOPK_MAX
    h = x.reshape(b * s, d)
    r3 = lambda t: t.reshape(b, s, t.shape[-1])
    for i in range(DEPTH):
        j = i // 2
        if i % 2 == 0:
            qlat, ckv, qidx, kidx, widx, qb, kb, vb = _proj_even(h, even_w_in[j], even_kv_norm[j], even_w_uk[j])
            o_a = _dsa_attention(r3(qlat), r3(qidx), r3(widx), r3(kidx), r3(ckv), even_w_uv[j])
            o_b = _stick_attention(r3(qb), r3(kb), r3(vb))
            w_out = even_w_out[j].astype(BF16)
            na = A_HEADS * HEAD_DIM
            outs = [o_a.reshape(b * s, -1), o_b.reshape(b * s, -1)]
            ws = [w_out[:na], w_out[na:]]
        else:
            nb = s // MOBA_BLOCK
            perm, srow, kf, vf = _moba_constants(nb, s)
            q, k, v, qfeat = _proj_odd(h, odd_w_in[j], perm, srow, nb)
            o_c = _moba_attention(r3(q), r3(qfeat), r3(k), r3(v), kf, vf)
            outs = [o_c.reshape(b * s, -1)]
            ws = [odd_w_out[j].astype(BF16)]
        h = _layer_tail(outs, ws, h, ln_mix_g[i], ln_mix_b[i],
                        ffn_gate[i], ffn_up[i], ffn_down[i], ln_ffn_g[i], ln_ffn_b[i])
    return h.reshape(b, s, d)
```

```python
import functools

import numpy as np
import jax
import jax.numpy as jnp
from jax import lax
from jax.experimental import pallas as pl
from jax.experimental.pallas import tpu as pltpu

D_MODEL = 1024
DEPTH = 2
HEAD_DIM = 64
A_HEADS = 8
A_KV_RANK = 128
IDX_HEADS = 4
IDX_DIM = 64
IDX_TOPK_MAX = 256
B_HEADS = 8
C_HEADS = 16
MOBA_BLOCK = 256
MOBA_TOPK = 3
D_FF = -(-8 * D_MODEL // (3 * 256)) * 256
DEEPNORM_ALPHA = (2 * DEPTH) ** 0.25
NEG = -1e30
LOG2E = 1.4426950408889634
SIGN_BIT = -2 ** 31

LANES = 128
VMEM_LIMIT = 52 * 1024 * 1024

BF16 = jnp.bfloat16
F32 = jnp.float32
I32 = jnp.int32


def _dot(a, b):
    return jnp.dot(a, b, preferred_element_type=F32)


def _dot_nt(a, b, precision=None):
    return lax.dot_general(a, b, (((1,), (1,)), ((), ())), precision=precision,
                           preferred_element_type=F32)


def _for_each(n, body):
    def two(j, carry):
        body(2 * j, carry)
        body(2 * j + 1, carry)
        return carry

    lax.fori_loop(0, n // 2, two, 0)

    @pl.when(n % 2 == 1)
    def _():
        body(n - 1, 0)


def _split_bf16(x):
    hi = x.astype(BF16)
    return hi, (x - hi.astype(F32)).astype(BF16)


def _iota(shape, dim):
    return lax.broadcasted_iota(I32, shape, dim)


def _params(sem, vmem=VMEM_LIMIT):
    return pltpu.CompilerParams(dimension_semantics=sem, vmem_limit_bytes=vmem)


def _const_spec(shape):
    nd = len(shape)
    return pl.BlockSpec(shape, lambda *_: (0,) * nd)


def _layer_norm(y, g, b):
    mu = jnp.mean(y, axis=-1, keepdims=True)
    d = y - mu
    var = jnp.mean(d * d, axis=-1, keepdims=True)
    return d * lax.rsqrt(var + 1e-5) * g + b


def _proj_even_kernel(x_ref, w_ref, wuk_ref, g_ref,
                      qlat_ref, ckv_ref, qidx_ref, kidx_ref, widx_ref, qb_ref, kb_ref, vb_ref):
    xb = x_ref[...].astype(BF16)
    qa = _dot(xb, w_ref[:, 0:512]).astype(BF16)
    for p in range(A_HEADS // 2):
        ql = _dot(qa[:, 128 * p:128 * (p + 1)], wuk_ref[p])
        qlat_ref[:, 256 * p:256 * (p + 1)] = (ql * HEAD_DIM ** -0.5).astype(BF16)
    c = _dot(xb, w_ref[:, 512:640])
    ms = jnp.mean(c * c, axis=-1, keepdims=True)
    ckv_ref[...] = (c * lax.rsqrt(ms + 1e-6) * g_ref[...]).astype(BF16)
    qidx_ref[...] = (_dot(xb, w_ref[:, 640:896]) * IDX_DIM ** -0.5).astype(BF16)
    kidx_ref[...] = _dot(xb, w_ref[:, 896:1024]).astype(BF16)
    widx_ref[...] = _dot(xb, w_ref[:, 1024:1152]) * IDX_HEADS ** -0.5
    qb_ref[...] = (_dot(xb, w_ref[:, 1152:1664]) * HEAD_DIM ** -0.5).astype(BF16)
    kb_ref[...] = _dot(xb, w_ref[:, 1664:2176]).astype(BF16)
    vb_ref[...] = _dot(xb, w_ref[:, 2176:2688]).astype(BF16)


def _proj_even(x2, w_in, kv_norm, w_uk, tm=512):
    n = x2.shape[0]
    cuts = np.cumsum([512, 128, 256, 64, 4, 512, 512, 512])[:-1].tolist()
    q_a, c_kv, q_idx, k_idx, w_idx, q_b, k_b, v_b = jnp.split(w_in, cuts, axis=1)
    w_idx = jnp.pad(w_idx, ((0, 0), (0, LANES - IDX_HEADS)))
    w = jnp.concatenate([q_a, c_kv, q_idx, k_idx, k_idx, w_idx, q_b, k_b, v_b], axis=1).astype(BF16)
    z = jnp.zeros((HEAD_DIM, A_KV_RANK), F32)
    wuk = jnp.stack([jnp.block([[w_uk[2 * p], z], [z, w_uk[2 * p + 1]]]) for p in range(A_HEADS // 2)]).astype(BF16)
    g = kv_norm.reshape(1, A_KV_RANK)
    widths = [(1024, BF16), (128, BF16), (256, BF16), (128, BF16), (128, F32), (512, BF16), (512, BF16), (512, BF16)]
    return pl.pallas_call(
        _proj_even_kernel,
        grid=(n // tm,),
        in_specs=[pl.BlockSpec((tm, D_MODEL), lambda i: (i, 0)),
                  _const_spec(w.shape), _const_spec(wuk.shape), _const_spec(g.shape)],
        out_specs=[pl.BlockSpec((tm, wd), lambda i: (i, 0)) for wd, _ in widths],
        out_shape=[jax.ShapeDtypeStruct((n, wd), dt) for wd, dt in widths],
        compiler_params=_params(("parallel",)),
    )(x2, w, wuk, g)


def _dsa_kernel(qlat_ref, qidx_ref, widx_ref, kidx_ref, ckv_ref, kfeat_ref, wuv_ref, o_ref,
                key_ref, msk_ref, ckv2_ref, logit_ref, mrun_ref, acc_ref, olat_ref, *, tq, topk):
    i = pl.program_id(1)
    t0 = i * tq
    tk = tq
    hq = tq // 2
    nt = key_ref.shape[0]
    n_pair = (i + 2) // 2
    int_min = -2 ** 31

    @pl.when(i == 0)
    def _():
        ckv2_ref[:, 0:LANES] = ckv_ref[0]
        ckv2_ref[:, LANES:2 * LANES] = kfeat_ref[...]

    rel = _iota((tk, tq), 0) - _iota((tk, tq), 1)
    lane_q = _iota((tq, LANES), 1)

    qm = []
    for h in range(IDX_HEADS):
        p, half = divmod(h, 2)
        qi = qidx_ref[0, :, 128 * p:128 * (p + 1)].astype(F32)
        qm.append(jnp.where((lane_q < IDX_DIM) == (half == 0), qi, 0.0).astype(BF16))
    w_t = widx_ref[0].T
    w_row = [w_t[h:h + 1, :] for h in range(IDX_HEADS)]

    def score_body(kp, carry):
        for u in range(2):
            kt = 2 * kp + u
            k0 = pl.multiple_of(kt * tk, tk)
            k2 = kidx_ref[0, pl.ds(k0, tk), :]
            acc = w_row[0] * jnp.maximum(_dot_nt(k2, qm[0]), 0.0)
            for h in range(1, IDX_HEADS):
                acc = acc + w_row[h] * jnp.maximum(_dot_nt(k2, qm[h]), 0.0)
            acc = jnp.where(rel <= t0 - k0, acc, NEG)
            bits = lax.bitcast_convert_type(acc, I32)
            key_ref[kt] = bits ^ ((bits >> 31) & 0x7FFFFFFF)
        return carry

    _for_each(n_pair, score_body)

    fold = 64

    def bit_body(bi, carry, cpass):
        cur, cfail = carry
        cand = cur + lax.shift_left(jnp.int32(1), 31 - bi)

        def cnt_body(kp, acc):
            for u in range(2):
                hit = jnp.where(key_ref[2 * kp + u] >= cand, 1.0, 0.0)
                acc = acc + jnp.sum(hit.reshape(tk // fold, fold, tq), axis=0)
            return acc

        acc = lax.fori_loop(0, n_pair, cnt_body, jnp.zeros((fold, tq), F32))
        cnt = jnp.sum(acc, axis=0, keepdims=True)
        ok = cnt >= topk
        return jnp.where(ok, cand, cur), jnp.where(ok, cfail, cnt), jnp.where(ok, cnt, cpass)

    def bit_body3(bi, carry):
        return bit_body(bi, carry[:2], carry[2])

    n_keys = (2 * n_pair * tk).astype(F32)
    thr, cnt_gt, cnt_ge = lax.fori_loop(
        0, 32, bit_body3,
        (jnp.full((1, tq), int_min, I32), jnp.zeros((1, tq), F32), jnp.zeros((1, tq), F32) + n_keys))
    need = topk - cnt_gt
    surplus = jnp.max(cnt_ge) > topk

    ltri = jnp.where(_iota((tk, tk), 0) >= _iota((tk, tk), 1), 1.0, 0.0).astype(BF16)

    def mask_plain(kp, carry):
        for u in range(2):
            kt = 2 * kp + u
            sel = (key_ref[kt] >= thr) & (rel <= t0 - kt * tk)
            msk_ref[kt] = jnp.where(sel, 0.0, NEG).T
        return carry

    def mask_body(kp, seen):
        for u in range(2):
            kt = 2 * kp + u
            kk = key_ref[kt]
            eq = kk == thr
            pc = _dot(ltri, jnp.where(eq, 1.0, 0.0).astype(BF16))
            take = eq & (pc + seen <= need)
            sel = ((kk > thr) | take) & (rel <= t0 - kt * tk)
            msk_ref[kt] = jnp.where(sel, 0.0, NEG).T
            seen = seen + pc[tk - 1:tk, :]
        return seen

    @pl.when(surplus)
    def _():
        lax.fori_loop(0, n_pair, mask_body, jnp.zeros((1, tq), F32))

    @pl.when(jnp.logical_not(surplus))
    def _():
        _for_each(n_pair, mask_plain)

    lane_h = _iota((hq, LANES), 1)
    for half in range(2):
        r0, r1 = half * hq, (half + 1) * hq
        t0h = (t0 + r0).astype(F32)
        parts = []
        for h in range(A_HEADS):
            slope = 2.0 ** (-8.0 * (h + 1) / A_HEADS)
            qf = jnp.where(lane_h < 2, slope, jnp.where(lane_h == 2, -slope * t0h, 0.0)).astype(BF16)
            parts.append(jnp.concatenate([qlat_ref[0, r0:r1, 128 * h:128 * (h + 1)], qf], axis=1))
        lhs = jnp.concatenate(parts, axis=0)
        mrun_ref[...] = jnp.full(mrun_ref.shape, NEG, F32)

        def pass1(kp, carry, lhs=lhs, r0=r0, r1=r1):
            k0 = pl.multiple_of(kp * 2 * tk, 2 * tk)
            s = _dot_nt(lhs, ckv2_ref[pl.ds(k0, 2 * tk), :])
            mk = [msk_ref[2 * kp + u, r0:r1, :] for u in range(2)]
            for h in range(A_HEADS):
                top = mrun_ref[h * hq:(h + 1) * hq, :]
                for u in range(2):
                    sh = s[h * hq:(h + 1) * hq, u * tk:(u + 1) * tk] + mk[u]
                    logit_ref[2 * kp + u, h * hq:(h + 1) * hq, :] = sh
                    top = jnp.maximum(top, jnp.maximum(sh[:, :LANES], sh[:, LANES:]))
                mrun_ref[h * hq:(h + 1) * hq, :] = top
            return carry

        _for_each(n_pair, pass1)
        m = jnp.max(mrun_ref[...], axis=-1, keepdims=True)
        acc_ref[...] = jnp.zeros(acc_ref.shape, F32)

        def pass2(kp, carry, m=m):
            p0 = jnp.exp(logit_ref[2 * kp] - m).astype(BF16)
            p1 = jnp.exp(logit_ref[2 * kp + 1] - m).astype(BF16)
            k0 = pl.multiple_of(kp * 2 * tk, 2 * tk)
            acc_ref[...] += _dot(jnp.concatenate([p0, p1], axis=1), ckv2_ref[pl.ds(k0, 2 * tk), :])
            return carry

        _for_each(n_pair, pass2)
        acc = acc_ref[...]
        o = acc[:, :LANES] / acc[:, LANES + 2:LANES + 3]
        for h in range(A_HEADS):
            olat_ref[r0:r1, 128 * h:128 * (h + 1)] = o[h * hq:(h + 1) * hq]

    for p in range(A_HEADS // 2):
        o = _dot(olat_ref[:, 256 * p:256 * (p + 1)].astype(BF16), wuv_ref[p])
        o_ref[0, :, 128 * p:128 * (p + 1)] = o.astype(BF16)


def _dsa_attention(qlat, qidx, widx, kidx, ckv, w_uv, tq=256):
    b, s, _ = qlat.shape
    topk = min(IDX_TOPK_MAX, s // 4)
    assert topk <= tq and s % (2 * tq) == 0
    z = jnp.zeros((A_KV_RANK, HEAD_DIM), F32)
    wuv = jnp.stack([jnp.block([[w_uv[2 * p], z], [z, w_uv[2 * p + 1]]]) for p in range(A_HEADS // 2)]).astype(BF16)
    pos = np.arange(s)
    kfeat = np.zeros((s, LANES), np.float32)
    kfeat[:, 0], kfeat[:, 1], kfeat[:, 2] = 256 * (pos // 256), pos % 256, 1.0
    kfeat = jnp.asarray(kfeat, BF16)
    nt = s // tq
    rows = A_HEADS * tq // 2
    return pl.pallas_call(
        functools.partial(_dsa_kernel, tq=tq, topk=topk),
        grid=(b, s // tq),
        in_specs=[pl.BlockSpec((1, tq, A_HEADS * A_KV_RANK), lambda bi, i: (bi, i, 0)),
                  pl.BlockSpec((1, tq, IDX_HEADS * IDX_DIM), lambda bi, i: (bi, i, 0)),
                  pl.BlockSpec((1, tq, LANES), lambda bi, i: (bi, i, 0)),
                  pl.BlockSpec((1, s, LANES), lambda bi, i: (bi, 0, 0)),
                  pl.BlockSpec((1, s, A_KV_RANK), lambda bi, i: (bi, 0, 0)),
                  _const_spec(kfeat.shape), _const_spec(wuv.shape)],
        out_specs=pl.BlockSpec((1, tq, A_HEADS * HEAD_DIM), lambda bi, i: (bi, i, 0)),
        out_shape=jax.ShapeDtypeStruct((b, s, A_HEADS * HEAD_DIM), BF16),
        scratch_shapes=[pltpu.VMEM((nt, tq, tq), I32),
                        pltpu.VMEM((nt, tq, tq), F32),
                        pltpu.VMEM((s, 2 * LANES), BF16),
                        pltpu.VMEM((nt, rows, tq), F32),
                        pltpu.VMEM((rows, LANES), F32),
                        pltpu.VMEM((rows, 2 * LANES), F32),
                        pltpu.VMEM((tq, A_HEADS * A_KV_RANK), F32)],
        compiler_params=_params(("parallel", "arbitrary")),
    )(qlat, qidx, widx, kidx, ckv, kfeat, wuv)


def _stick_kernel(q_ref, k_ref, v_ref, tri_ref, dm_ref, o_ref, kbd_ref, vbd_ref, c_ref, acc_ref, z_ref, a_ref, *, tq):
    i = pl.program_id(2)
    t0 = i * tq
    w2 = 2 * LANES
    nt = kbd_ref.shape[0] // w2
    sub = tq // LANES
    assert sub % 2 == 0

    @pl.when(i == 0)
    def _():
        lo = _iota((LANES, LANES), 1) < HEAD_DIM

        def build(kt, carry):
            r0 = pl.multiple_of(kt * LANES, LANES)
            d0 = pl.multiple_of(kt * w2, w2)
            kk = k_ref[0, pl.ds(r0, LANES), :].astype(F32)
            vv = v_ref[0, pl.ds(r0, LANES), :].astype(F32)
            kbd_ref[pl.ds(d0, LANES), :] = jnp.where(lo, kk, 0.0).astype(BF16)
            kbd_ref[pl.ds(d0 + LANES, LANES), :] = jnp.where(lo, 0.0, kk).astype(BF16)
            vbd_ref[pl.ds(d0, LANES), :] = jnp.where(lo, vv, 0.0).astype(BF16)
            vbd_ref[pl.ds(d0 + LANES, LANES), :] = jnp.where(lo, 0.0, vv).astype(BF16)
            return carry

        lax.fori_loop(0, nt, build, 0)

    q = q_ref[0]
    tri = tri_ref[...]
    c_ref[...] = jnp.zeros(c_ref.shape, F32)
    acc_ref[...] = jnp.zeros(acc_ref.shape, F32)

    def rows_of(kt):
        start = kt * w2
        return pl.ds(start if isinstance(start, int) else pl.multiple_of(start, 2 * w2), 2 * w2)

    def scores(kt, r0=0):
        return _dot_nt(q[r0:], kbd_ref[rows_of(kt), :])

    def weights(z, kt, masked, r0=0):
        nr = tq - r0
        z = z * LOG2E
        neg_abs = lax.bitcast_convert_type(lax.bitcast_convert_type(z, I32) | SIGN_BIT, F32)
        t = jnp.maximum(z, 0.0) + jnp.log2(1.0 + jnp.exp2(neg_abs))
        log_beta = z - t
        if masked:
            strict = dm_ref[0:nr, :]
            t = t * strict
        hi = t.astype(BF16)
        lo = (t - hi.astype(F32)).astype(BF16)
        lhs = jnp.concatenate(
            [jnp.concatenate([hi[:, g * LANES:(g + 1) * LANES], lo[:, g * LANES:(g + 1) * LANES]], axis=1)
             for g in range(4)], axis=0)
        r = _dot(lhs, tri)
        part = lambda g, c0: r[g * nr:(g + 1) * nr, c0:c0 + LANES]
        c_hi = c_ref[r0:, :]
        c_lo = c_hi + jnp.concatenate([part(2, LANES), part(3, LANES)], axis=1)
        rem = jnp.concatenate([jnp.concatenate([part(0, 0), part(1, 0)], axis=1) + c_lo,
                               jnp.concatenate([part(2, 0), part(3, 0)], axis=1) + c_hi], axis=1)
        a = jnp.exp2(log_beta + rem)
        if masked:
            a = a * strict
        c_ref[r0:, :] = c_lo + jnp.concatenate([part(0, LANES), part(1, LANES)], axis=1)
        return a.astype(BF16)

    def accumulate(a, kt, r0=0):
        acc_ref[r0:, :] += _dot(a, vbd_ref[rows_of(kt), :])

    n_diag = sub // 2
    for d in range(n_diag):
        kt = (i + 1) * sub - 2 - 2 * d
        r0 = (sub - 2 - 2 * d) * LANES
        a = weights(scores(kt, r0), kt, True, r0)
        if d < n_diag - 1:
            accumulate(a, kt, r0)
        else:
            a_ref[...] = a

    z_ref[...] = scores(jnp.maximum(i * sub - 2, 0))

    def body(j, carry):
        for u in range(n_diag):
            kt = i * sub - 2 - 2 * (n_diag * j + u)
            z = z_ref[...]
            z_ref[...] = scores(jnp.maximum(kt - 2, 0))
            accumulate(a_ref[...], kt + 2)
            a_ref[...] = weights(z, kt, False)
        return carry

    lax.fori_loop(0, i, body, 0)
    accumulate(a_ref[...], 0)
    o_ref[0] = acc_ref[...].astype(BF16)


def _stick_attention(q, k, v, tq=512):
    b, s, w = q.shape
    jj = np.arange(2 * LANES)[:, None] % LANES
    ss = np.arange(2 * LANES)[None, :]
    tri = jnp.asarray(-((ss >= LANES) | (jj > ss)).astype(np.float32), BF16)
    nt = s // LANES
    col = np.arange(4 * LANES)[None, :]
    dmask = jnp.asarray(((col // (2 * LANES)) * LANES + col % LANES < np.arange(tq)[:, None]).astype(np.float32))
    return pl.pallas_call(
        functools.partial(_stick_kernel, tq=tq),
        grid=(b, w // LANES, s // tq),
        in_specs=[pl.BlockSpec((1, tq, LANES), lambda bi, p, i: (bi, i, p)),
                  pl.BlockSpec((1, s, LANES), lambda bi, p, i: (bi, 0, p)),
                  pl.BlockSpec((1, s, LANES), lambda bi, p, i: (bi, 0, p)),
                  _const_spec(tri.shape), _const_spec(dmask.shape)],
        out_specs=pl.BlockSpec((1, tq, LANES), lambda bi, p, i: (bi, i, p)),
        out_shape=jax.ShapeDtypeStruct((b, s, w), BF16),
        scratch_shapes=[pltpu.VMEM((nt * 2 * LANES, LANES), BF16),
                        pltpu.VMEM((nt * 2 * LANES, LANES), BF16),
                        pltpu.VMEM((tq, 2 * LANES), F32),
                        pltpu.VMEM((tq, LANES), F32),
                        pltpu.VMEM((tq, 4 * LANES), F32),
                        pltpu.VMEM((tq, 4 * LANES), BF16)],
        compiler_params=_params(("parallel", "parallel", "arbitrary")),
    )(q, k, v, tri, dmask)


def _layer_tail_kernel(*refs, n_in, fc):
    o_refs, w_refs = refs[:n_in], refs[n_in:2 * n_in]
    x_ref, g1_ref, b1_ref, wg_ref, wu_ref, wd_ref, g2_ref, b2_ref, out_ref, acc_ref = refs[2 * n_in:]
    m = _dot(o_refs[0][...], w_refs[0][...])
    for o_ref, w_ref in zip(o_refs[1:], w_refs[1:]):
        m = m + _dot(o_ref[...], w_ref[...])
    h = _layer_norm(DEEPNORM_ALPHA * x_ref[...] + m, g1_ref[...], b1_ref[...])
    hb = h.astype(BF16)
    for c in range(D_FF // fc):
        gate = _dot(hb, wg_ref[:, c * fc:(c + 1) * fc])
        up = _dot(hb, wu_ref[:, c * fc:(c + 1) * fc])
        act = (gate * jax.nn.sigmoid(gate) * up).astype(BF16)
        d = _dot(act, wd_ref[c * fc:(c + 1) * fc, :])
        if c == 0:
            acc_ref[...] = d
        else:
            acc_ref[...] += d
    out_ref[...] = _layer_norm(DEEPNORM_ALPHA * h + acc_ref[...], g2_ref[...], b2_ref[...])


def _layer_tail(os_, ws, x2, g1, b1, wg, wu, wd, g2, b2, tm=512, fc=256):
    n = x2.shape[0]
    n_in = len(os_)
    single = dict(pipeline_mode=pl.Buffered(1))
    vec = lambda v: v.reshape(1, D_MODEL)
    return pl.pallas_call(
        functools.partial(_layer_tail_kernel, n_in=n_in, fc=fc),
        grid=(n // tm,),
        in_specs=([pl.BlockSpec((tm, o.shape[1]), lambda i: (i, 0)) for o in os_]
                  + [pl.BlockSpec(w.shape, lambda i: (0, 0), **single) for w in ws]
                  + [pl.BlockSpec((tm, D_MODEL), lambda i: (i, 0)),
                     _const_spec((1, D_MODEL)), _const_spec((1, D_MODEL)),
                     pl.BlockSpec((D_MODEL, D_FF), lambda i: (0, 0), **single),
                     pl.BlockSpec((D_MODEL, D_FF), lambda i: (0, 0), **single),
                     pl.BlockSpec((D_FF, D_MODEL), lambda i: (0, 0), **single),
                     _const_spec((1, D_MODEL)), _const_spec((1, D_MODEL))]),
        out_specs=pl.BlockSpec((tm, D_MODEL), lambda i: (i, 0)),
        out_shape=jax.ShapeDtypeStruct((n, D_MODEL), F32),
        scratch_shapes=[pltpu.VMEM((tm, D_MODEL), F32)],
        compiler_params=_params(("parallel",)),
    )(*os_, *ws, x2, vec(g1), vec(b1), wg.astype(BF16), wu.astype(BF16), wd.astype(BF16), vec(g2), vec(b2))


MOBA_SEL_LANE = 0
MOBA_POS_LANE = 32
MOBA_GROUP = 2


def _proj_odd_kernel(x_ref, w_ref, perm_ref, slope_ref, q_ref, k_ref, v_ref, qf_ref, kme_ref, *, nb):
    step = pl.program_id(0)
    n = step % nb
    nh = C_HEADS
    wq = nh * HEAD_DIM

    @pl.when(step == 0)
    def _():
        kme_ref[...] = jnp.zeros_like(kme_ref)

    xb = x_ref[...].astype(BF16)
    qf = _dot(xb, w_ref[:, 0:wq])
    kf = _dot(xb, w_ref[:, wq:2 * wq])
    q_ref[...] = (qf * HEAD_DIM ** -0.5).astype(BF16)
    k_ref[...] = kf.astype(BF16)
    v_ref[...] = _dot(xb, w_ref[:, 2 * wq:3 * wq]).astype(BF16)

    k_hi, k_lo = _split_bf16(kme_ref[...])
    q_hi, q_lo = _split_bf16(qf)
    gate = _dot_nt(k_hi, q_hi) + _dot_nt(k_hi, q_lo) + _dot_nt(k_lo, q_hi)
    blk = _iota(gate.shape, 0) // nh
    gate = jnp.where(blk < n, gate, NEG)
    gs = [gate[nh * m:nh * (m + 1), :] for m in range(nb)]
    biases = []
    for a in range(nb):
        rank = jnp.zeros_like(gs[a])
        for m in range(nb):
            if m != a:
                beats = (gs[m] >= gs[a]) if m < a else (gs[m] > gs[a])
                rank = rank + jnp.where(beats, 1.0, 0.0)
        biases.append(jnp.where(rank < MOBA_TOPK, 0.0, NEG))
    bias = jnp.where(blk < n, jnp.concatenate(biases, axis=0), jnp.where(blk == n, 0.0, NEG))
    qf_ref[...] = (_dot(bias.T.astype(BF16), perm_ref[...]) + slope_ref[...]).astype(BF16)

    kmean = jnp.mean(kf, axis=0, keepdims=True)
    head = _iota(kmean.shape, 1) // HEAD_DIM
    for h in range(nh):
        kme_ref[pl.ds(n * nh + h, 1), :] = jnp.where(head == h, kmean, 0.0)


def _bf16_pieces(x):
    x = np.float32(x)
    out = []
    for _ in range(3):
        p = np.float32(np.asarray(x, dtype=BF16))
        out.append(p)
        x = np.float32(x - p)
    assert x == 0.0
    return out


def _moba_constants(nb, s):
    nh, p = C_HEADS, MOBA_BLOCK
    assert 2 * nb <= MOBA_POS_LANE and MOBA_POS_LANE + 12 <= LANES
    slopes = np.asarray(2.0 ** (-8.0 * np.arange(1, nh + 1) / nh), np.float32)
    perm = np.zeros((nb * nh, nh // 2 * LANES), np.float32)
    srow = np.zeros((1, nh // 2 * LANES), np.float32)
    for h in range(nh):
        base = (h // 2) * LANES
        for m in range(nb):
            perm[m * nh + h, base + MOBA_SEL_LANE + (h % 2) * nb + m] = 1.0
        pieces = _bf16_pieces(slopes[h])
        for c in range(6):
            srow[0, base + MOBA_POS_LANE + 6 * (h % 2) + c] = pieces[c % 3]
    pos = np.arange(s)
    kf = np.zeros((2, s, LANES), np.float32)
    for half in range(2):
        kf[half, pos, MOBA_SEL_LANE + half * nb + pos // p] = 1.0
        for c in range(3):
            kf[half, :, MOBA_POS_LANE + 6 * half + c] = pos % p
            kf[half, :, MOBA_POS_LANE + 6 * half + 3 + c] = p * (pos // p)
    vf = np.zeros((LANES, 2 * p), np.float32)
    vf[0, :p] = 1.0
    vf[1, p:] = 1.0
    return (jnp.asarray(perm, BF16), jnp.asarray(srow, F32), jnp.asarray(kf, BF16), jnp.asarray(vf, BF16))


def _proj_odd(x2, w_in, perm, srow, nb):
    n = x2.shape[0]
    p = MOBA_BLOCK
    wq = C_HEADS * HEAD_DIM
    w = w_in.astype(BF16)
    fw = perm.shape[1]
    return pl.pallas_call(
        functools.partial(_proj_odd_kernel, nb=nb),
        grid=(n // p,),
        in_specs=[pl.BlockSpec((p, D_MODEL), lambda i: (i, 0)), _const_spec(w.shape),
                  _const_spec(perm.shape), _const_spec(srow.shape)],
        out_specs=[pl.BlockSpec((p, wq), lambda i: (i, 0))] * 3 + [pl.BlockSpec((p, fw), lambda i: (i, 0))],
        out_shape=[jax.ShapeDtypeStruct((n, wq), BF16)] * 3 + [jax.ShapeDtypeStruct((n, fw), BF16)],
        scratch_shapes=[pltpu.VMEM((nb * C_HEADS, wq), F32)],
        compiler_params=_params(("arbitrary",)),
    )(x2, w, perm, srow)


def _moba_kernel(q_ref, qf_ref, k_ref, v_ref, kf_ref, vf_ref, o_ref, kbd_ref, vbt_ref, s_ref, m_ref, acc_ref, *, nb):
    i = pl.program_id(2)
    p = MOBA_BLOCK

    @pl.when(i == 0)
    def _():
        lo = _iota((p, LANES), 1) < HEAD_DIM

        def build(n, carry):
            r0 = pl.multiple_of(n * p, p)
            d0 = pl.multiple_of(n * 2 * p, 2 * p)
            kk = k_ref[0, pl.ds(r0, p), :].astype(F32)
            vv = v_ref[0, pl.ds(r0, p), :].astype(F32)
            kbd_ref[pl.ds(d0, p), 0:LANES] = jnp.where(lo, kk, 0.0).astype(BF16)
            kbd_ref[pl.ds(d0, p), LANES:2 * LANES] = kf_ref[0, pl.ds(r0, p), :]
            kbd_ref[pl.ds(d0 + p, p), 0:LANES] = jnp.where(lo, 0.0, kk).astype(BF16)
            kbd_ref[pl.ds(d0 + p, p), LANES:2 * LANES] = kf_ref[1, pl.ds(r0, p), :]
            v_t = vv.T
            top = _iota(v_t.shape, 0) < HEAD_DIM
            vbt_ref[0:LANES, pl.ds(d0, p)] = jnp.where(top, v_t, 0.0).astype(BF16)
            vbt_ref[0:LANES, pl.ds(d0 + p, p)] = jnp.where(top, 0.0, v_t).astype(BF16)
            vbt_ref[LANES:2 * LANES, pl.ds(d0, 2 * p)] = vf_ref[...]
            return carry

        lax.fori_loop(0, nb, build, 0)

    tq = MOBA_GROUP * p
    gw = MOBA_GROUP * 2 * p
    lhs = jnp.concatenate([q_ref[0], qf_ref[0]], axis=1)

    def cols(g):
        return pl.ds(pl.multiple_of(g * gw, gw), gw)

    def scores(g):
        return _dot_nt(kbd_ref[cols(g), :], lhs)

    row_a = _iota((2 * LANES, tq), 0)
    head0 = (row_a < HEAD_DIM) | (row_a == LANES)

    def attend(g, s, masked):
        m0, m1 = m_ref[0:1, :], m_ref[1:2, :]
        if masked:
            row = _iota((gw, tq), 0)
            key = ((row >> 9) << 8) + (row & (p - 1))
            s = jnp.where(key <= _iota((gw, tq), 1), s, NEG)
        s0 = [s[u * 2 * p:u * 2 * p + p] for u in range(MOBA_GROUP)]
        s1 = [s[u * 2 * p + p:(u + 1) * 2 * p] for u in range(MOBA_GROUP)]
        n0 = jnp.maximum(m0, jnp.max(functools.reduce(jnp.maximum, s0), axis=0, keepdims=True))
        n1 = jnp.maximum(m1, jnp.max(functools.reduce(jnp.maximum, s1), axis=0, keepdims=True))
        pexp = jnp.concatenate(
            [e for u in range(MOBA_GROUP) for e in (jnp.exp(s0[u] - n0), jnp.exp(s1[u] - n1))], axis=0)
        alpha = jnp.where(head0, jnp.exp(m0 - n0), jnp.exp(m1 - n1))
        m_ref[0:1, :] = n0
        m_ref[1:2, :] = n1
        acc_ref[...] = acc_ref[...] * alpha + _dot(vbt_ref[:, cols(g)], pexp.astype(BF16))

    m_ref[...] = jnp.full(m_ref.shape, NEG, F32)
    acc_ref[...] = jnp.zeros(acc_ref.shape, F32)
    s_ref[...] = scores(0)

    def step(g):
        s = s_ref[...]
        s_ref[...] = scores(g + 1)
        attend(g, s, False)

    def body(j, carry):
        step(2 * j)
        step(2 * j + 1)
        return carry

    lax.fori_loop(0, i // 2, body, 0)

    @pl.when(i % 2 == 1)
    def _():
        step(i - 1)

    attend(i, s_ref[...], True)
    acc = acc_ref[...]
    l = jnp.where(_iota((LANES, tq), 0) < HEAD_DIM, acc[LANES:LANES + 1, :], acc[LANES + 1:LANES + 2, :])
    o_ref[0] = (acc[:LANES, :] / l).T.astype(BF16)


def _moba_attention(q, qfeat, k, v, kf, vf):
    b, s, w = q.shape
    p = MOBA_BLOCK
    nb = s // p
    tq = MOBA_GROUP * p
    assert nb % MOBA_GROUP == 0
    return pl.pallas_call(
        functools.partial(_moba_kernel, nb=nb),
        grid=(b, w // LANES, s // tq),
        in_specs=[pl.BlockSpec((1, tq, LANES), lambda bi, hp, i: (bi, i, hp)),
                  pl.BlockSpec((1, tq, LANES), lambda bi, hp, i: (bi, i, hp)),
                  pl.BlockSpec((1, s, LANES), lambda bi, hp, i: (bi, 0, hp)),
                  pl.BlockSpec((1, s, LANES), lambda bi, hp, i: (bi, 0, hp)),
                  _const_spec(kf.shape), _const_spec(vf.shape)],
        out_specs=pl.BlockSpec((1, tq, LANES), lambda bi, hp, i: (bi, i, hp)),
        out_shape=jax.ShapeDtypeStruct((b, s, w), BF16),
        scratch_shapes=[pltpu.VMEM((nb * 2 * p, 2 * LANES), BF16),
                        pltpu.VMEM((2 * LANES, nb * 2 * p), BF16),
                        pltpu.VMEM((MOBA_GROUP * 2 * p, tq), F32),
                        pltpu.VMEM((8, tq), F32),
                        pltpu.VMEM((2 * LANES, tq), F32)],
        compiler_params=_params(("parallel", "parallel", "arbitrary")),
    )(q, qfeat, k, v, kf, vf)


def kernel(x, even_w_in, even_kv_norm, even_w_uk, even_w_uv, even_w_out, odd_w_in, odd_w_out,
           ln_mix_g, ln_mix_b, ffn_gate, ffn_up, ffn_down, ln_ffn_g, ln_ffn_b):
    b, s, d = x.shape
    assert d == D_MODEL and s % (2 * MOBA_BLOCK) == 0 and s >= 4 * IDX_TOPK_MAX
    h = x.reshape(b * s, d)
    r3 = lambda t: t.reshape(b, s, t.shape[-1])
    for i in range(DEPTH):
        j = i // 2
        if i % 2 == 0:
            qlat, ckv, qidx, kidx, widx, qb, kb, vb = _proj_even(h, even_w_in[j], even_kv_norm[j], even_w_uk[j])
            o_a = _dsa_attention(r3(qlat), r3(qidx), r3(widx), r3(kidx), r3(ckv), even_w_uv[j])
            o_b = _stick_attention(r3(qb), r3(kb), r3(vb))
            w_out = even_w_out[j].astype(BF16)
            na = A_HEADS * HEAD_DIM
            outs = [o_a.reshape(b * s, -1), o_b.reshape(b * s, -1)]
            ws = [w_out[:na], w_out[na:]]
        else:
            nb = s // MOBA_BLOCK
            perm, srow, kf, vf = _moba_constants(nb, s)
            q, k, v, qfeat = _proj_odd(h, odd_w_in[j], perm, srow, nb)
            o_c = _moba_attention(r3(q), r3(qfeat), r3(k), r3(v), kf, vf)
            outs = [o_c.reshape(b * s, -1)]
            ws = [odd_w_out[j].astype(BF16)]
        h = _layer_tail(outs, ws, h, ln_mix_g[i], ln_mix_b[i],
                        ffn_gate[i], ffn_up[i], ffn_down[i], ln_ffn_g[i], ln_ffn_b[i])
    return h.reshape(b, s, d)
```
